```python
import jax, jax.numpy as jnp
from jax import lax
import numpy as np

D_MODEL = 4096
BATCH = 1
SEQ = 8192
DEPTH = 1
DEC_BATCH = 32
DEC_SEQ = 32
PAST_LEN = 2048

CHUNK = 64
CONV_WIDTH = 31
CONV_DIM = D_MODEL // 2
HEAD_DIM = 128
N_HEADS = (D_MODEL // 2) // HEAD_DIM
ATTN_DIM = N_HEADS * HEAD_DIM
N_PREV_CHUNKS = 8
BAND_ROWS = N_PREV_CHUNKS * CHUNK
BAND = BAND_ROWS + CHUNK
REL_CLIP = 256
N_REL = 2 * REL_CLIP + 1
ATTN_SCALE = HEAD_DIM ** -0.5
MIX_DIM = CONV_DIM + ATTN_DIM
IN_DIM = 2 * CONV_DIM + 3 * ATTN_DIM
PEER_HEADS = 8
PEER_KEY_DIM = 256
PEER_HALF = PEER_KEY_DIM // 2
N_KEYS = 128
N_EXPERTS = N_KEYS * N_KEYS
PEER_TOPK = 16
EPS = 1e-6
NEG_INF = -1e30

kernel_name = "hybrid_conv_chunkattn_peer_step"


def _rms_norm(x, g):
    xf = x.astype(jnp.float32)
    y = xf * lax.rsqrt(jnp.mean(xf * xf, axis=-1, keepdims=True) + EPS)
    return (y * g.astype(jnp.float32)).astype(x.dtype)


def _layer_norm(x, g, b):
    xf = x.astype(jnp.float32)
    mu = jnp.mean(xf, axis=-1, keepdims=True)
    var = jnp.mean(jnp.square(xf - mu), axis=-1, keepdims=True)
    y = (xf - mu) * lax.rsqrt(var + EPS)
    return (y * g.astype(jnp.float32) + b.astype(jnp.float32)).astype(x.dtype)


def _in_proj(x, norm_g, w_in, b_in):
    bsz, t = x.shape[:2]
    z = jnp.einsum('btd,de->bte', _rms_norm(x, norm_g), w_in) + b_in
    o = 2 * CONV_DIM
    u = z[..., :CONV_DIM] * jax.nn.sigmoid(z[..., CONV_DIM:o])
    q = z[..., o:o + ATTN_DIM].reshape(bsz, t, N_HEADS, HEAD_DIM)
    k = z[..., o + ATTN_DIM:o + 2 * ATTN_DIM].reshape(bsz, t, N_HEADS, HEAD_DIM)
    v = z[..., o + 2 * ATTN_DIM:].reshape(bsz, t, N_HEADS, HEAD_DIM)
    return u, q, k, v


def _conv_module(u_padded, w_dw, b_dw, ln_g, ln_b):
    y = lax.conv_general_dilated(
        u_padded, w_dw[:, None, :].astype(u_padded.dtype), window_strides=(1,), padding='VALID',
        dimension_numbers=('NWC', 'WIO', 'NWC'), feature_group_count=CONV_DIM) + b_dw
    return jax.nn.silu(_layer_norm(y, ln_g, ln_b))


def _rel_bias(rel_bias, n_q, n_hist, n_k):
    dist = jnp.arange(n_q)[:, None] + n_hist - jnp.arange(n_k)[None, :]
    idx = jnp.clip(dist, -REL_CLIP, REL_CLIP) + REL_CLIP
    return rel_bias[:, idx].astype(jnp.float32)


def _attend(q, k, v, bias, valid=None):
    s = jnp.einsum('bqhd,bkhd->bhqk', q * ATTN_SCALE, k).astype(jnp.float32) + bias
    if valid is not None:
        s = jnp.where(valid, s, NEG_INF)
    p = jax.nn.softmax(s, axis=-1).astype(v.dtype)
    return jnp.einsum('bhqk,bkhd->bqhd', p, v)


def _band_attention_prompt(q, k, v, rel_bias):
    bsz, t = q.shape[:2]
    nc = t // CHUNK
    zpad = jnp.zeros((bsz, BAND_ROWS, N_HEADS, HEAD_DIM), k.dtype)
    kp = jnp.concatenate([zpad, k], axis=1)
    vp = jnp.concatenate([zpad, v], axis=1)
    qc = q.reshape(bsz, nc, CHUNK, N_HEADS, HEAD_DIM)
    bias = _rel_bias(rel_bias, CHUNK, BAND_ROWS, BAND)

    def one_chunk(c):
        q_c = lax.dynamic_index_in_dim(qc, c, axis=1, keepdims=False)
        k_c = lax.dynamic_slice_in_dim(kp, c * CHUNK, BAND, axis=1)
        v_c = lax.dynamic_slice_in_dim(vp, c * CHUNK, BAND, axis=1)
        valid = (c * CHUNK + jnp.arange(BAND)) >= BAND_ROWS
        return _attend(q_c, k_c, v_c, bias, valid[None, None, None, :])

    o = lax.map(one_chunk, jnp.arange(nc))
    return jnp.moveaxis(o, 0, 1).reshape(bsz, t, ATTN_DIM)


def _band_attention_sample(q, k, v, cache_k, cache_v, rel_bias):
    bsz, s = q.shape[:2]
    r = cache_k.shape[1]
    kc = jnp.concatenate([cache_k, k], axis=1)
    vc = jnp.concatenate([cache_v, v], axis=1)
    bias = _rel_bias(rel_bias, s, r, r + s)
    return _attend(q, kc, vc, bias).reshape(bsz, s, ATTN_DIM)


def _merge(conv_y, attn_y, g_conv, g_attn, w_out):
    h = jnp.concatenate([_rms_norm(conv_y, g_conv), _rms_norm(attn_y, g_attn)], axis=-1)
    return jnp.einsum('btc,cd->btd', h, w_out)


def _peer_block(x, norm_g, w_q, sub_keys, peer_u, peer_v):
    bsz, t, d = x.shape
    h = _rms_norm(x, norm_g)
    blk = CHUNK if t % CHUNK == 0 else t
    hb = h.reshape(-1, blk, d)

    def one_block(hm):
        m = hm.shape[0]
        q = (hm @ w_q).reshape(m, PEER_HEADS, 2, PEER_HALF)
        s = jnp.einsum('mhpc,hpnc->mhpn', q, sub_keys)
        s_top, i_top = lax.top_k(s, PEER_TOPK)
        cand_s = (s_top[:, :, 0, :, None] + s_top[:, :, 1, None, :]).reshape(m, PEER_HEADS, PEER_TOPK * PEER_TOPK)
        cand_i = (i_top[:, :, 0, :, None] * N_KEYS + i_top[:, :, 1, None, :]).reshape(m, PEER_HEADS, PEER_TOPK * PEER_TOPK)
        best_s, pos = lax.top_k(cand_s, PEER_TOPK)
        idx = jnp.take_along_axis(cand_i, pos, axis=-1)
        gate = jax.nn.softmax(best_s.astype(jnp.float32), axis=-1)
        u_sel = peer_u[idx]
        act = jax.nn.gelu(jnp.einsum('mhkd,md->mhk', u_sel, hm).astype(jnp.float32), approximate=False)
        a = (gate * act).astype(hm.dtype)
        return jnp.einsum('mhk,mhkd->md', a, peer_v[idx])

    return lax.map(one_block, hb).reshape(bsz, t, d)


def setup_inputs(seed: int = 0) -> dict:
    key = jax.random.key(seed)
    ks = jax.random.split(key, 24)
    f32 = jnp.float32

    def nrm(k, shape, scale):
        return jax.random.normal(k, shape, f32) * scale

    cache_rows = min(BAND_ROWS, PAST_LEN)
    return {
        "x_prompt": nrm(ks[0], (BATCH, SEQ, D_MODEL), 1.0),
        "x_sample": nrm(ks[1], (DEC_BATCH, DEC_SEQ, D_MODEL), 1.0),
        "cache_conv": nrm(ks[2], (DEPTH, DEC_BATCH, CONV_WIDTH - 1, CONV_DIM), 0.5),
        "cache_k": nrm(ks[3], (DEPTH, DEC_BATCH, cache_rows, N_HEADS, HEAD_DIM), 1.0),
        "cache_v": nrm(ks[4], (DEPTH, DEC_BATCH, cache_rows, N_HEADS, HEAD_DIM), 1.0),
        "mix_norm_g": 1.0 + nrm(ks[5], (DEPTH, D_MODEL), 0.01),
        "w_in": nrm(ks[6], (DEPTH, D_MODEL, IN_DIM), D_MODEL ** -0.5),
        "b_in": nrm(ks[7], (DEPTH, IN_DIM), 0.01),
        "w_dw": nrm(ks[8], (DEPTH, CONV_WIDTH, CONV_DIM), CONV_WIDTH ** -0.5),
        "b_dw": nrm(ks[9], (DEPTH, CONV_DIM), 0.01),
        "ln_g": 1.0 + nrm(ks[10], (DEPTH, CONV_DIM), 0.01),
        "ln_b": nrm(ks[11], (DEPTH, CONV_DIM), 0.01),
        "rel_bias": nrm(ks[12], (DEPTH, N_HEADS, N_REL), 0.5),
        "out_norm_conv_g": 1.0 + nrm(ks[13], (DEPTH, CONV_DIM), 0.01),
        "out_norm_attn_g": 1.0 + nrm(ks[14], (DEPTH, ATTN_DIM), 0.01),
        "w_out": nrm(ks[15], (DEPTH, MIX_DIM, D_MODEL), MIX_DIM ** -0.5),
        "ffn_norm_g": 1.0 + nrm(ks[16], (DEPTH, D_MODEL), 0.01),
        "peer_w_q": nrm(ks[17], (DEPTH, D_MODEL, PEER_HEADS * PEER_KEY_DIM), D_MODEL ** -0.5),
        "peer_sub_keys": nrm(ks[18], (DEPTH, PEER_HEADS, 2, N_KEYS, PEER_HALF), PEER_HALF ** -0.5),
        "peer_u": nrm(ks[19], (DEPTH, N_EXPERTS, D_MODEL), D_MODEL ** -0.5),
        "peer_v": nrm(ks[20], (DEPTH, N_EXPERTS, D_MODEL), PEER_HEADS ** -0.5),
        "final_norm_g": 1.0 + nrm(ks[21], (D_MODEL,), 0.01),
    }


def reference(x_prompt, x_sample, cache_conv, cache_k, cache_v, mix_norm_g, w_in, b_in, w_dw, b_dw,
              ln_g, ln_b, rel_bias, out_norm_conv_g, out_norm_attn_g, w_out, ffn_norm_g,
              peer_w_q, peer_sub_keys, peer_u, peer_v, final_norm_g):
    xp, xs = x_prompt, x_sample
    bsz, t = xp.shape[:2]
    new_conv_p, new_k_p, new_v_p = [], [], []
    new_conv_s, new_k_s, new_v_s = [], [], []
    for l in range(DEPTH):
        u_p, q_p, k_p, v_p = _in_proj(xp, mix_norm_g[l], w_in[l], b_in[l])
        u_p_pad = jnp.concatenate([jnp.zeros((bsz, CONV_WIDTH - 1, CONV_DIM), u_p.dtype), u_p], axis=1)
        conv_p = _conv_module(u_p_pad, w_dw[l], b_dw[l], ln_g[l], ln_b[l])
        attn_p = _band_attention_prompt(q_p, k_p, v_p, rel_bias[l])
        xp = xp + _merge(conv_p, attn_p, out_norm_conv_g[l], out_norm_attn_g[l], w_out[l])
        rows_p = min(BAND_ROWS, t)
        new_conv_p.append(u_p_pad[:, -(CONV_WIDTH - 1):])
        new_k_p.append(k_p[:, -rows_p:])
        new_v_p.append(v_p[:, -rows_p:])
        u_s, q_s, k_s, v_s = _in_proj(xs, mix_norm_g[l], w_in[l], b_in[l])
        u_s_pad = jnp.concatenate([cache_conv[l].astype(u_s.dtype), u_s], axis=1)
        conv_s = _conv_module(u_s_pad, w_dw[l], b_dw[l], ln_g[l], ln_b[l])
        attn_s = _band_attention_sample(q_s, k_s, v_s, cache_k[l].astype(k_s.dtype),
                                        cache_v[l].astype(v_s.dtype), rel_bias[l])
        xs = xs + _merge(conv_s, attn_s, out_norm_conv_g[l], out_norm_attn_g[l], w_out[l])
        new_conv_s.append(u_s_pad[:, -(CONV_WIDTH - 1):])
        new_k_s.append(k_s)
        new_v_s.append(v_s)
        xp = xp + _peer_block(xp, ffn_norm_g[l], peer_w_q[l], peer_sub_keys[l], peer_u[l], peer_v[l])
        xs = xs + _peer_block(xs, ffn_norm_g[l], peer_w_q[l], peer_sub_keys[l], peer_u[l], peer_v[l])
    y_prompt = _rms_norm(xp, final_norm_g)
    y_sample = _rms_norm(xs, final_norm_g)
    return (y_prompt, y_sample,
            jnp.stack(new_conv_p), jnp.stack(new_k_p), jnp.stack(new_v_p),
            jnp.stack(new_conv_s), jnp.stack(new_k_s), jnp.stack(new_v_s))
```

```python
import functools

import numpy as np
import jax
import jax.numpy as jnp
from jax import lax
from jax.experimental import pallas as pl
from jax.experimental.pallas import tpu as pltpu

F32 = jnp.float32
BF16 = jnp.bfloat16

CHUNK = 64
CONV_WIDTH = 31
CONV_HALO = 32
HEAD_DIM = 128
N_PREV_CHUNKS = 8
BAND_ROWS = N_PREV_CHUNKS * CHUNK
REL_CLIP = 256
ATTN_SCALE = HEAD_DIM ** -0.5
PEER_HEADS = 8
PEER_HALF = 128
N_KEYS = 128
PEER_TOPK = 16
SUBLANES = 8
EPS = 1e-6
NEG_INF = -1e30

ATTN_Q_TILE = 2 * CHUNK
ATTN_WINDOW = BAND_ROWS + ATTN_Q_TILE
TOEPLITZ_N = 768

VMEM_LIMIT = 56 * 1024 * 1024


def _cparams(sem):
    return pltpu.CompilerParams(dimension_semantics=sem, vmem_limit_bytes=VMEM_LIMIT)


def _rms(x, g):
    ms = jnp.mean(x * x, axis=-1, keepdims=True)
    return x * lax.rsqrt(ms + EPS) * g


def _norm_matmul_kernel(x_ref, g_ref, w_ref, b_ref, o_ref, *rest, emit_norm):
    if emit_norm:
        hn_ref, xn_ref = rest
    else:
        (xn_ref,) = rest

    @pl.when(pl.program_id(1) == 0)
    def _():
        xn = _rms(x_ref[...], g_ref[...]).astype(BF16)
        xn_ref[...] = xn
        if emit_norm:
            hn_ref[...] = xn

    o_ref[...] = jnp.dot(xn_ref[...], w_ref[...], preferred_element_type=F32) + b_ref[...]


def _norm_matmul(x, g, w, b, *, tm, tn, emit_norm=False):
    t, d = x.shape
    n = w.shape[1]
    tm = min(tm, t)
    assert t % tm == 0 and n % tn == 0
    out_shape = [jax.ShapeDtypeStruct((t, n), F32)]
    out_specs = [pl.BlockSpec((tm, tn), lambda i, j: (i, j))]
    if emit_norm:
        out_shape.append(jax.ShapeDtypeStruct((t, d), BF16))
        out_specs.append(pl.BlockSpec((tm, d), lambda i, j: (i, 0)))
    res = pl.pallas_call(
        functools.partial(_norm_matmul_kernel, emit_norm=emit_norm),
        grid=(t // tm, n // tn),
        in_specs=[
            pl.BlockSpec((tm, d), lambda i, j: (i, 0)),
            pl.BlockSpec((1, d), lambda i, j: (0, 0)),
            pl.BlockSpec((d, tn), lambda i, j: (0, j)),
            pl.BlockSpec((1, tn), lambda i, j: (0, j)),
        ],
        out_specs=out_specs,
        out_shape=out_shape,
        scratch_shapes=[pltpu.VMEM((tm, d), BF16)],
        compiler_params=_cparams(("parallel", "arbitrary")),
        name="norm_matmul_emit" if emit_norm else "norm_matmul",
    )(x, g, w, b)
    return res if emit_norm else res[0]


def _conv_kernel(val_ref, gate_ref, hist_ref, w_ref, bdw_ref, lng_ref, lnb_ref, gc_ref,
                 u_ref, o_ref, upad_ref, *, tt):
    t = pl.program_id(1)

    @pl.when(t == 0)
    def _():
        upad_ref[0:CONV_HALO, :] = hist_ref[0]

    @pl.when(t > 0)
    def _():
        upad_ref[0:CONV_HALO, :] = upad_ref[tt:tt + CONV_HALO, :]

    u = val_ref[...] * jax.nn.sigmoid(gate_ref[...])
    u_ref[...] = u
    upad_ref[CONV_HALO:CONV_HALO + tt, :] = u

    rows = 8
    for c in range(tt // rows):
        r0 = c * rows
        acc = jnp.broadcast_to(bdw_ref[...], (rows, bdw_ref.shape[1]))
        for k in range(CONV_WIDTH):
            lo = r0 + (CONV_HALO - (CONV_WIDTH - 1)) + k
            acc = acc + upad_ref[lo:lo + rows, :] * w_ref[k:k + 1, :]
        mu = jnp.mean(acc, axis=-1, keepdims=True)
        cen = acc - mu
        var = jnp.mean(cen * cen, axis=-1, keepdims=True)
        yn = cen * lax.rsqrt(var + EPS) * lng_ref[...] + lnb_ref[...]
        s = yn * jax.nn.sigmoid(yn)
        o_ref[r0:r0 + rows, :] = _rms(s, gc_ref[...]).astype(BF16)


def _conv_module(z, hist, w_dw, b_dw, ln_g, ln_b, g_conv, *, n_streams, tt):
    t_all = z.shape[0]
    c = w_dw.shape[1]
    t_stream = t_all // n_streams
    tt = min(tt, t_stream)
    nt = t_stream // tt
    return pl.pallas_call(
        functools.partial(_conv_kernel, tt=tt),
        grid=(n_streams, nt),
        in_specs=[
            pl.BlockSpec((tt, c), lambda b, t: (b * nt + t, 0)),
            pl.BlockSpec((tt, c), lambda b, t: (b * nt + t, 1)),
            pl.BlockSpec((1, CONV_HALO, c), lambda b, t: (b, 0, 0)),
            pl.BlockSpec((CONV_WIDTH, c), lambda b, t: (0, 0)),
            pl.BlockSpec((1, c), lambda b, t: (0, 0)),
            pl.BlockSpec((1, c), lambda b, t: (0, 0)),
            pl.BlockSpec((1, c), lambda b, t: (0, 0)),
            pl.BlockSpec((1, c), lambda b, t: (0, 0)),
        ],
        out_specs=[
            pl.BlockSpec((tt, c), lambda b, t: (b * nt + t, 0)),
            pl.BlockSpec((tt, c), lambda b, t: (b * nt + t, 0)),
        ],
        out_shape=[jax.ShapeDtypeStruct((t_all, c), F32), jax.ShapeDtypeStruct((t_all, c), BF16)],
        scratch_shapes=[pltpu.VMEM((CONV_HALO + tt, c), F32)],
        compiler_params=_cparams(("arbitrary", "arbitrary")),
        name="conv_module",
    )(z, z, hist, w_dw, b_dw, ln_g, ln_b, g_conv)


def _bias_row_index():
    m = np.arange(TOEPLITZ_N)
    dist = np.where(m <= TOEPLITZ_N - ATTN_Q_TILE, BAND_ROWS - m, BAND_ROWS + TOEPLITZ_N - m)
    return np.clip(dist, -REL_CLIP, REL_CLIP) + REL_CLIP


def _toeplitz(frow, rows):
    x = jnp.broadcast_to(frow, (rows, frow.shape[-1]))
    return pltpu.roll(x, 0, 1, stride=1, stride_axis=0)


def _attn_prompt_kernel(q_ref, k_ref, v_ref, f_ref, o_ref, kbuf_ref, vbuf_ref, bm_ref, *, t):
    i = pl.program_id(1)

    @pl.when(i == 0)
    def _():
        zeros = jnp.zeros((BAND_ROWS, HEAD_DIM), BF16)
        kbuf_ref[0:BAND_ROWS, :] = zeros
        vbuf_ref[0:BAND_ROWS, :] = zeros
        kbuf_ref[BAND_ROWS:BAND_ROWS + t, :] = k_ref[...].astype(BF16)
        vbuf_ref[BAND_ROWS:BAND_ROWS + t, :] = v_ref[...].astype(BF16)
        bias = _toeplitz(f_ref[0], ATTN_Q_TILE)[:, :ATTN_WINDOW]
        qc = lax.broadcasted_iota(jnp.int32, (ATTN_Q_TILE, ATTN_WINDOW), 0) // CHUNK
        kc = lax.broadcasted_iota(jnp.int32, (ATTN_Q_TILE, ATTN_WINDOW), 1) // CHUNK
        band = (kc >= qc) & (kc <= qc + N_PREV_CHUNKS)
        bm_ref[...] = jnp.where(band, bias, NEG_INF)

    start = pl.multiple_of(i * ATTN_Q_TILE, ATTN_Q_TILE)
    q = (q_ref[...] * ATTN_SCALE).astype(BF16)
    kw = kbuf_ref[pl.ds(start, ATTN_WINDOW), :]
    vw = vbuf_ref[pl.ds(start, ATTN_WINDOW), :]
    s = lax.dot_general(q, kw, (((1,), (1,)), ((), ())), preferred_element_type=F32) + bm_ref[...]
    col = lax.broadcasted_iota(jnp.int32, (ATTN_Q_TILE, ATTN_WINDOW), 1)
    s = jnp.where(col >= BAND_ROWS - start, s, NEG_INF)
    m = jnp.max(s, axis=-1, keepdims=True)
    p = jnp.exp(s - m)
    l = jnp.sum(p, axis=-1, keepdims=True)
    o = jnp.dot(p.astype(BF16), vw, preferred_element_type=F32)
    o_ref[...] = o / l


def _attn_prompt(z, frow, *, n_heads, q_blk, k_blk, v_blk):
    t = z.shape[0]
    return pl.pallas_call(
        functools.partial(_attn_prompt_kernel, t=t),
        grid=(n_heads, t // ATTN_Q_TILE),
        in_specs=[
            pl.BlockSpec((ATTN_Q_TILE, HEAD_DIM), lambda h, i: (i, q_blk + h)),
            pl.BlockSpec((t, HEAD_DIM), lambda h, i: (0, k_blk + h)),
            pl.BlockSpec((t, HEAD_DIM), lambda h, i: (0, v_blk + h)),
            pl.BlockSpec((1, 1, TOEPLITZ_N), lambda h, i: (h, 0, 0)),
        ],
        out_specs=pl.BlockSpec((ATTN_Q_TILE, HEAD_DIM), lambda h, i: (i, h)),
        out_shape=jax.ShapeDtypeStruct((t, n_heads * HEAD_DIM), F32),
        scratch_shapes=[
            pltpu.VMEM((BAND_ROWS + t, HEAD_DIM), BF16),
            pltpu.VMEM((BAND_ROWS + t, HEAD_DIM), BF16),
            pltpu.VMEM((ATTN_Q_TILE, ATTN_WINDOW), F32),
        ],
        compiler_params=_cparams(("arbitrary", "arbitrary")),
        name="attn_prompt",
    )(z, z, z, frow)


def _attn_sample_kernel(q_ref, k_ref, v_ref, ck_ref, cv_ref, f_ref, o_ref, bias_ref, *, n_heads, s_len, r_len):
    @pl.when(pl.program_id(0) == 0)
    def _():
        for h in range(n_heads):
            bias_ref[h] = _toeplitz(f_ref[h], s_len)

    for h in range(n_heads):
        hs = slice(h * HEAD_DIM, (h + 1) * HEAD_DIM)
        q = (q_ref[:, hs] * ATTN_SCALE).astype(BF16)
        kc = ck_ref[0, :, hs].astype(BF16)
        vc = cv_ref[0, :, hs].astype(BF16)
        kn = k_ref[:, hs].astype(BF16)
        vn = v_ref[:, hs].astype(BF16)
        nt = (((1,), (1,)), ((), ()))
        off = BAND_ROWS - r_len
        s_c = lax.dot_general(q, kc, nt, preferred_element_type=F32) + bias_ref[h, :, off:off + r_len]
        s_n = lax.dot_general(q, kn, nt, preferred_element_type=F32) + bias_ref[h, :, BAND_ROWS:BAND_ROWS + s_len]
        m = jnp.maximum(jnp.max(s_c, axis=-1, keepdims=True), jnp.max(s_n, axis=-1, keepdims=True))
        p_c = jnp.exp(s_c - m)
        p_n = jnp.exp(s_n - m)
        l = jnp.sum(p_c, axis=-1, keepdims=True) + jnp.sum(p_n, axis=-1, keepdims=True)
        o = (jnp.dot(p_c.astype(BF16), vc, preferred_element_type=F32)
             + jnp.dot(p_n.astype(BF16), vn, preferred_element_type=F32))
        o_ref[:, hs] = o / l


def _attn_sample(z, cache_k, cache_v, frow, *, n_streams, n_heads, q_blk, k_blk, v_blk):
    s_len = z.shape[0] // n_streams
    r_len = cache_k.shape[1]
    a = n_heads * HEAD_DIM
    return pl.pallas_call(
        functools.partial(_attn_sample_kernel, n_heads=n_heads, s_len=s_len, r_len=r_len),
        grid=(n_streams,),
        in_specs=[
            pl.BlockSpec((s_len, a), lambda b: (b, q_blk)),
            pl.BlockSpec((s_len, a), lambda b: (b, k_blk)),
            pl.BlockSpec((s_len, a), lambda b: (b, v_blk)),
            pl.BlockSpec((1, r_len, a), lambda b: (b, 0, 0)),
            pl.BlockSpec((1, r_len, a), lambda b: (b, 0, 0)),
            pl.BlockSpec((n_heads, 1, TOEPLITZ_N), lambda b: (0, 0, 0)),
        ],
        out_specs=pl.BlockSpec((s_len, a), lambda b: (b, 0)),
        out_shape=jax.ShapeDtypeStruct((z.shape[0], a), F32),
        scratch_shapes=[pltpu.VMEM((n_heads, s_len, TOEPLITZ_N), F32)],
        compiler_params=_cparams(("arbitrary",)),
        name="attn_sample",
    )(z, z, z, cache_k, cache_v, frow)


def _merge_kernel(cn_ref, at_ref, ga_ref, w_ref, x_ref, o_ref, h_ref):
    c = cn_ref.shape[1]

    @pl.when(pl.program_id(1) == 0)
    def _():
        h_ref[:, 0:c] = cn_ref[...]
        h_ref[:, c:] = _rms(at_ref[...], ga_ref[...]).astype(BF16)

    o_ref[...] = x_ref[...] + jnp.dot(h_ref[...], w_ref[...], preferred_element_type=F32)


def _merge(conv_n, attn, g_attn, w_out, x, *, tm, tn):
    t, d = x.shape
    c = conv_n.shape[1]
    a = attn.shape[1]
    tm = min(tm, t)
    return pl.pallas_call(
        _merge_kernel,
        grid=(t // tm, d // tn),
        in_specs=[
            pl.BlockSpec((tm, c), lambda i, j: (i, 0)),
            pl.BlockSpec((tm, a), lambda i, j: (i, 0)),
            pl.BlockSpec((1, a), lambda i, j: (0, 0)),
            pl.BlockSpec((c + a, tn), lambda i, j: (0, j)),
            pl.BlockSpec((tm, tn), lambda i, j: (i, j)),
        ],
        out_specs=pl.BlockSpec((tm, tn), lambda i, j: (i, j)),
        out_shape=jax.ShapeDtypeStruct((t, d), F32),
        scratch_shapes=[pltpu.VMEM((tm, c + a), BF16)],
        compiler_params=_cparams(("parallel", "arbitrary")),
        name="merge",
    )(conv_n, attn, g_attn, w_out, x)


def _top16(s, n_rows):
    lanes = s.shape[1]
    row = lax.broadcasted_iota(jnp.int32, (n_rows, lanes), 0)
    slot = lax.broadcasted_iota(jnp.int32, (PEER_TOPK, lanes), 0)

    def body(k, carry):
        s, rank, vals = carry
        m = jnp.max(s, axis=0, keepdims=True)
        idx = jnp.min(jnp.where(s == m, row, n_rows), axis=0, keepdims=True)
        hit = row == idx
        rank = jnp.where(hit, k, rank)
        s = jnp.where(hit, -jnp.inf, s)
        vals = jnp.where(slot == k, m, vals)
        return s, rank, vals

    init = (s, jnp.full((n_rows, lanes), PEER_TOPK, jnp.int32), jnp.zeros((PEER_TOPK, lanes), F32))
    _, rank, vals = lax.fori_loop(0, PEER_TOPK, body, init)
    return vals, rank


def _peer_select_kernel(q_ref, keys_ref, cnt_ref, e1_ref, r2_ref, e2_ref):
    lanes = q_ref.shape[0]
    slot = lax.broadcasted_iota(jnp.int32, (PEER_TOPK, lanes), 0)
    nt = (((1,), (1,)), ((), ()))
    for h in range(PEER_HEADS):
        halves = []
        for p in range(2):
            c0 = (h * 2 + p) * PEER_HALF
            qhp = q_ref[:, c0:c0 + PEER_HALF].astype(BF16)
            s = lax.dot_general(keys_ref[h, p], qhp, nt, preferred_element_type=F32)
            vals, rank = _top16(s, N_KEYS)
            halves.append((s, vals, rank))
        (s1, v1, rank1), (s2, v2, rank2) = halves
        cand = jnp.concatenate([v1[a:a + 1, :] + v2 for a in range(PEER_TOPK)], axis=0)
        best, crank = _top16(cand, PEER_TOPK * PEER_TOPK)
        z = jnp.sum(jnp.exp(best - best[0:1, :]), axis=0, keepdims=True)
        kept = (crank < PEER_TOPK).astype(jnp.int32)
        cnt = jnp.zeros((N_KEYS, lanes), jnp.int32)
        for a in range(PEER_TOPK):
            cnt_a = jnp.sum(kept[a * PEER_TOPK:(a + 1) * PEER_TOPK, :], axis=0, keepdims=True)
            cnt = jnp.where(rank1 == a, cnt_a, cnt)
        cnt_ref[h] = cnt.astype(F32)
        e1_ref[h] = jnp.exp(s1 - v1[0:1, :]) / z
        r2_ref[h] = rank2.astype(F32)
        e2_ref[h] = jnp.exp(s2 - v2[0:1, :])


def _peer_select(q, sub_keys, *, tq):
    t = q.shape[0]
    tq = min(tq, t)
    tab = jax.ShapeDtypeStruct((PEER_HEADS, N_KEYS, t), F32)
    tab_spec = pl.BlockSpec((PEER_HEADS, N_KEYS, tq), lambda i: (0, 0, i))
    return pl.pallas_call(
        _peer_select_kernel,
        grid=(t // tq,),
        in_specs=[
            pl.BlockSpec((tq, q.shape[1]), lambda i: (i, 0)),
            pl.BlockSpec(sub_keys.shape, lambda i: (0, 0, 0, 0)),
        ],
        out_specs=[tab_spec] * 4,
        out_shape=[tab] * 4,
        compiler_params=_cparams(("parallel",)),
        name="peer_select",
    )(q, sub_keys)


def _peer_ffn_kernel(hn_ref, u_ref, v_ref, cnt_ref, e1_ref, r2_ref, e2_ref, o_ref, a_ref, *, rows_per_tile):
    j = pl.program_id(1)

    @pl.when(j == 0)
    def _():
        o_ref[...] = jnp.zeros_like(o_ref)

    nt = (((1,), (1,)), ((), ()))
    act = lax.dot_general(u_ref[...], hn_ref[...], nt, preferred_element_type=F32)
    for r in range(rows_per_tile):
        row = (j * rows_per_tile) % SUBLANES + r
        gate = jnp.zeros((N_KEYS, act.shape[1]), F32)
        for h in range(PEER_HEADS):
            cnt = cnt_ref[h, pl.ds(row, 1), :]
            e1 = e1_ref[h, pl.ds(row, 1), :]
            gate = gate + jnp.where(r2_ref[h] < cnt, e2_ref[h], 0.0) * e1
        x = act[r * N_KEYS:(r + 1) * N_KEYS, :]
        gelu = 0.5 * x * (1.0 + lax.erf(x * (2.0 ** -0.5)))
        a_ref[r * N_KEYS:(r + 1) * N_KEYS, :] = (gate * gelu).astype(BF16)
    tn = (((0,), (0,)), ((), ()))
    o_ref[...] += lax.dot_general(a_ref[...], v_ref[...], tn, preferred_element_type=F32)


def _peer_ffn(hn, u_bf, v_bf, tables, *, tm, te):
    t, d = hn.shape
    n_exp = u_bf.shape[0]
    tm = min(tm, t)
    rows_per_tile = te // N_KEYS
    assert SUBLANES % rows_per_tile == 0
    tab_spec = pl.BlockSpec((PEER_HEADS, N_KEYS, tm), lambda i, j: (0, 0, i))
    row_spec = pl.BlockSpec((PEER_HEADS, SUBLANES, tm), lambda i, j: (0, (j * rows_per_tile) // SUBLANES, i))
    return pl.pallas_call(
        functools.partial(_peer_ffn_kernel, rows_per_tile=rows_per_tile),
        grid=(t // tm, n_exp // te),
        in_specs=[
            pl.BlockSpec((tm, d), lambda i, j: (i, 0)),
            pl.BlockSpec((te, d), lambda i, j: (j, 0)),
            pl.BlockSpec((te, d), lambda i, j: (j, 0)),
            row_spec, row_spec, tab_spec, tab_spec,
        ],
        out_specs=pl.BlockSpec((tm, d), lambda i, j: (i, 0)),
        out_shape=jax.ShapeDtypeStruct((t, d), F32),
        scratch_shapes=[pltpu.VMEM((te, tm), BF16)],
        compiler_params=_cparams(("parallel", "arbitrary")),
        name="peer_ffn",
    )(hn, u_bf, v_bf, *tables)


def _final_kernel(x_ref, p_ref, g_ref, o_ref):
    o_ref[...] = _rms(x_ref[...] + p_ref[...], g_ref[...])


def _final(x1, peer, g, *, tm):
    t, d = x1.shape
    tm = min(tm, t)
    spec = pl.BlockSpec((tm, d), lambda i: (i, 0))
    return pl.pallas_call(
        _final_kernel,
        grid=(t // tm,),
        in_specs=[spec, spec, pl.BlockSpec((1, d), lambda i: (0, 0))],
        out_specs=spec,
        out_shape=jax.ShapeDtypeStruct((t, d), F32),
        compiler_params=_cparams(("parallel",)),
        name="final_norm",
    )(x1, peer, g)


def _group(x, hist, cache_kv, p, *, n_streams):
    c = p["w_dw"].shape[1]
    a = p["g_attn"].shape[1]
    n_heads = a // HEAD_DIM
    z = _norm_matmul(x, p["mix_g"], p["w_in"], p["b_in"], tm=512, tn=512)
    u, conv_n = _conv_module(z, hist, p["w_dw"], p["b_dw"], p["ln_g"], p["ln_b"], p["g_conv"],
                             n_streams=n_streams, tt=128)
    if cache_kv is None:
        blk = 2 * c // HEAD_DIM
        attn = _attn_prompt(z, p["frow"], n_heads=n_heads, q_blk=blk, k_blk=blk + n_heads, v_blk=blk + 2 * n_heads)
    else:
        blk = 2 * c // a
        attn = _attn_sample(z, cache_kv[0], cache_kv[1], p["frow"], n_streams=n_streams, n_heads=n_heads,
                            q_blk=blk, k_blk=blk + 1, v_blk=blk + 2)
    x1 = _merge(conv_n, attn, p["g_attn"], p["w_out"], x, tm=512, tn=512)
    q, hn = _norm_matmul(x1, p["ffn_g"], p["w_q"], p["zero_bq"], tm=512, tn=512, emit_norm=True)
    tables = _peer_select(q, p["sub_keys"], tq=128)
    peer = _peer_ffn(hn, p["peer_u"], p["peer_v"], tables, tm=512, te=512)
    y = _final(x1, peer, p["final_g"], tm=256)
    return y, z, u


def kernel(x_prompt, x_sample, cache_conv, cache_k, cache_v, mix_norm_g, w_in, b_in, w_dw, b_dw, ln_g, ln_b,
           rel_bias, out_norm_conv_g, out_norm_attn_g, w_out, ffn_norm_g, peer_w_q, peer_sub_keys, peer_u, peer_v,
           final_norm_g):
    assert mix_norm_g.shape[0] == 1, "single-layer step"
    bsz, seq, d = x_prompt.shape
    db, ds, _ = x_sample.shape
    c = w_dw.shape[2]
    a = out_norm_attn_g.shape[1]
    n_heads = a // HEAD_DIM
    assert bsz == 1 and seq % ATTN_Q_TILE == 0 and ds >= CONV_WIDTH - 1

    row = lambda v: v.reshape(1, -1)
    p = dict(
        mix_g=row(mix_norm_g[0]), w_in=w_in[0].astype(BF16), b_in=row(b_in[0]),
        w_dw=w_dw[0], b_dw=row(b_dw[0]), ln_g=row(ln_g[0]), ln_b=row(ln_b[0]),
        g_conv=row(out_norm_conv_g[0]), g_attn=row(out_norm_attn_g[0]), w_out=w_out[0].astype(BF16),
        ffn_g=row(ffn_norm_g[0]), w_q=peer_w_q[0].astype(BF16),
        zero_bq=jnp.zeros((1, peer_w_q.shape[2]), F32),
        sub_keys=peer_sub_keys[0].astype(BF16), peer_u=peer_u[0].astype(BF16), peer_v=peer_v[0].astype(BF16),
        final_g=row(final_norm_g),
        frow=rel_bias[0][:, _bias_row_index()].reshape(n_heads, 1, TOEPLITZ_N),
    )

    hist_p = jnp.zeros((1, CONV_HALO, c), F32)
    y_p, z_p, u_p = _group(x_prompt.reshape(seq, d), hist_p, None, p, n_streams=1)

    pad = CONV_HALO - (CONV_WIDTH - 1)
    hist_s = jnp.pad(cache_conv[0], ((0, 0), (pad, 0), (0, 0)))
    r_len = cache_k.shape[2]
    cache_kv = (cache_k[0].reshape(db, r_len, a), cache_v[0].reshape(db, r_len, a))
    y_s, z_s, u_s = _group(x_sample.reshape(db * ds, d), hist_s, cache_kv, p, n_streams=db)

    k0, v0 = 2 * c + a, 2 * c + 2 * a
    rows_p = min(BAND_ROWS, seq)
    keep = CONV_WIDTH - 1
    return (
        y_p.reshape(1, seq, d),
        y_s.reshape(db, ds, d),
        u_p[seq - keep:].reshape(1, 1, keep, c),
        z_p[seq - rows_p:, k0:k0 + a].reshape(1, 1, rows_p, n_heads, HEAD_DIM),
        z_p[seq - rows_p:, v0:v0 + a].reshape(1, 1, rows_p, n_heads, HEAD_DIM),
        u_s.reshape(db, ds, c)[:, ds - keep:].reshape(1, db, keep, c),
        z_s[:, k0:k0 + a].reshape(1, db, ds, n_heads, HEAD_DIM),
        z_s[:, v0:v0 + a].reshape(1, db, ds, n_heads, HEAD_DIM),
    )
```

```python
import functools

import numpy as np
import jax
import jax.numpy as jnp
from jax import lax
from jax.experimental import pallas as pl
from jax.experimental.pallas import tpu as pltpu

F32 = jnp.float32
BF16 = jnp.bfloat16

CHUNK = 64
CONV_WIDTH = 31
CONV_HALO = 32
HEAD_DIM = 128
N_PREV_CHUNKS = 8
BAND_ROWS = N_PREV_CHUNKS * CHUNK
REL_CLIP = 256
ATTN_SCALE = HEAD_DIM ** -0.5
PEER_HEADS = 8
PEER_HALF = 128
N_KEYS = 128
PEER_TOPK = 16
SUBLANES = 8
MXU_WIDTH = 256
EPS = 1e-6
NEG_INF = -1e30

ATTN_Q_TILE = 2 * CHUNK
ATTN_WINDOW = BAND_ROWS + ATTN_Q_TILE
TOEPLITZ_N = 768

VMEM_LIMIT = 56 * 1024 * 1024

TILES = dict(
    proj_rows=1024, proj_cols=512,
    conv_rows=128,
    attn_rows=512,
    select_tokens=128,
    ffn_tokens=512, ffn_experts=512,
    final_rows=256,
)


def _cparams(sem, flags=None):
    return pltpu.CompilerParams(dimension_semantics=sem, vmem_limit_bytes=VMEM_LIMIT, flags=flags)


def _rms(x, g):
    ms = jnp.mean(x * x, axis=-1, keepdims=True)
    return x * lax.rsqrt(ms + EPS) * g


NORM_ROWS = 16


def _norm_rows_to(dst_ref, n_rows, row_fn):
    def step(r, carry):
        rows = pl.ds(pl.multiple_of(r * NORM_ROWS, NORM_ROWS), NORM_ROWS)
        dst_ref[rows, :] = row_fn(rows).astype(BF16)
        return carry

    lax.fori_loop(0, n_rows // NORM_ROWS, step, 0)


def _norm_matmul_kernel(x_ref, g_ref, w_ref, b_ref, o_ref, xn_ref):
    @pl.when(pl.program_id(1) == 0)
    def _():
        _norm_rows_to(xn_ref, x_ref.shape[0], lambda rows: _rms(x_ref[rows, :], g_ref[...]))

    o_ref[...] = jnp.dot(xn_ref[...], w_ref[...], preferred_element_type=F32) + b_ref[...]


def _once_per_row_tile(block_shape, index_map):
    return pl.BlockSpec(block_shape, index_map, pipeline_mode=pl.Buffered(1))


def _norm_matmul(x, g, w, b, *, tm, tn, emit_norm=False):
    t, d = x.shape
    n = w.shape[1]
    tm = min(tm, t)
    assert t % tm == 0 and n % tn == 0 and tm % NORM_ROWS == 0
    out_shape = [jax.ShapeDtypeStruct((t, n), F32)]
    out_specs = [pl.BlockSpec((tm, tn), lambda i, j: (i, j))]
    scratch = [pltpu.VMEM((tm, d), BF16)]
    if emit_norm:
        out_shape.append(jax.ShapeDtypeStruct((t, d), BF16))
        out_specs.append(pl.BlockSpec((tm, d), lambda i, j: (i, 0)))
        scratch = []
    res = pl.pallas_call(
        _norm_matmul_kernel,
        grid=(t // tm, n // tn),
        in_specs=[
            _once_per_row_tile((tm, d), lambda i, j: (i, 0)),
            pl.BlockSpec((1, d), lambda i, j: (0, 0)),
            pl.BlockSpec((d, tn), lambda i, j: (0, j)),
            pl.BlockSpec((1, tn), lambda i, j: (0, j)),
        ],
        out_specs=out_specs,
        out_shape=out_shape,
        scratch_shapes=scratch,
        compiler_params=_cparams(("parallel", "arbitrary")),
        name="norm_matmul_emit" if emit_norm else "norm_matmul",
    )(x, g, w, b)
    return res if emit_norm else res[0]


def _conv_kernel(val_ref, gate_ref, hist_ref, w_ref, bdw_ref, lng_ref, lnb_ref, gc_ref,
                 u_ref, o_ref, ush_ref, par_ref, *, tt):
    t = pl.program_id(1)
    n_rows = CONV_HALO + tt
    n_ch = w_ref.shape[1]

    @pl.when((pl.program_id(0) == 0) & (t == 0))
    def _():
        for k in range(CONV_WIDTH):
            par_ref[k] = jnp.broadcast_to(w_ref[k:k + 1, :], (SUBLANES, n_ch))
        for k, ref in enumerate((bdw_ref, lng_ref, lnb_ref, gc_ref)):
            par_ref[CONV_WIDTH + k] = jnp.broadcast_to(ref[...], (SUBLANES, n_ch))

    @pl.when(t == 0)
    def _():
        ush_ref[0, 0:CONV_HALO, :] = hist_ref[0]

    @pl.when(t > 0)
    def _():
        ush_ref[0, 0:CONV_HALO, :] = ush_ref[0, tt:tt + CONV_HALO, :]

    u = val_ref[...] * jax.nn.sigmoid(gate_ref[...])
    u_ref[...] = u
    ush_ref[0, CONV_HALO:n_rows, :] = u
    upad = ush_ref[0]
    for s in range(1, SUBLANES):
        ush_ref[s] = pltpu.roll(upad, n_rows - s, 0)

    rows = SUBLANES
    for c in range(tt // rows):
        r0 = c * rows
        acc = par_ref[CONV_WIDTH]
        for k in range(CONV_WIDTH):
            off = CONV_HALO - (CONV_WIDTH - 1) + k
            lo = r0 + off - off % SUBLANES
            acc = acc + ush_ref[off % SUBLANES, lo:lo + rows, :] * par_ref[k]
        mu = jnp.mean(acc, axis=-1, keepdims=True)
        cen = acc - mu
        var = jnp.mean(cen * cen, axis=-1, keepdims=True)
        yn = cen * lax.rsqrt(var + EPS) * par_ref[CONV_WIDTH + 1] + par_ref[CONV_WIDTH + 2]
        s = yn * jax.nn.sigmoid(yn)
        o_ref[r0:r0 + rows, :] = _rms(s, par_ref[CONV_WIDTH + 3]).astype(BF16)


def _conv_module(z, hist, w_dw, b_dw, ln_g, ln_b, g_conv, *, n_streams, tt):
    t_all = z.shape[0]
    c = w_dw.shape[1]
    t_stream = t_all // n_streams
    tt = min(tt, t_stream)
    nt = t_stream // tt
    return pl.pallas_call(
        functools.partial(_conv_kernel, tt=tt),
        grid=(n_streams, nt),
        in_specs=[
            pl.BlockSpec((tt, c), lambda b, t: (b * nt + t, 0)),
            pl.BlockSpec((tt, c), lambda b, t: (b * nt + t, 1)),
            pl.BlockSpec((1, CONV_HALO, c), lambda b, t: (b, 0, 0)),
            pl.BlockSpec((CONV_WIDTH, c), lambda b, t: (0, 0)),
            pl.BlockSpec((1, c), lambda b, t: (0, 0)),
            pl.BlockSpec((1, c), lambda b, t: (0, 0)),
            pl.BlockSpec((1, c), lambda b, t: (0, 0)),
            pl.BlockSpec((1, c), lambda b, t: (0, 0)),
        ],
        out_specs=[
            pl.BlockSpec((tt, c), lambda b, t: (b * nt + t, 0)),
            pl.BlockSpec((tt, c), lambda b, t: (b * nt + t, 0)),
        ],
        out_shape=[jax.ShapeDtypeStruct((t_all, c), F32), jax.ShapeDtypeStruct((t_all, c), BF16)],
        scratch_shapes=[pltpu.VMEM((SUBLANES, CONV_HALO + tt, c), F32),
                        pltpu.VMEM((CONV_WIDTH + 4, SUBLANES, c), F32)],
        compiler_params=_cparams(("arbitrary", "arbitrary")),
        name="conv_module",
    )(z, z, hist, w_dw, b_dw, ln_g, ln_b, g_conv)


def _bias_row_index():
    m = np.arange(TOEPLITZ_N)
    dist = np.where(m <= TOEPLITZ_N - ATTN_Q_TILE, BAND_ROWS - m, BAND_ROWS + TOEPLITZ_N - m)
    return np.clip(dist, -REL_CLIP, REL_CLIP) + REL_CLIP


def _toeplitz(frow, rows):
    x = jnp.broadcast_to(frow, (rows, frow.shape[-1]))
    return pltpu.roll(x, 0, 1, stride=1, stride_axis=0)


def _attn_prompt_kernel(q_ref, k_ref, v_ref, f_ref, o_ref, kbuf_ref, vbuf_ref, bm_ref, *, t):
    i = pl.program_id(1)

    @pl.when(i == 0)
    def _():
        zeros = jnp.zeros((BAND_ROWS, HEAD_DIM), BF16)
        kbuf_ref[0:BAND_ROWS, :] = zeros
        vbuf_ref[0:BAND_ROWS, :] = zeros
        kbuf_ref[BAND_ROWS:BAND_ROWS + t, :] = k_ref[...].astype(BF16)
        vbuf_ref[BAND_ROWS:BAND_ROWS + t, :] = v_ref[...].astype(BF16)
        bias = _toeplitz(f_ref[0], ATTN_Q_TILE)[:, :ATTN_WINDOW]
        qc = lax.broadcasted_iota(jnp.int32, (ATTN_Q_TILE, ATTN_WINDOW), 0) // CHUNK
        kc = lax.broadcasted_iota(jnp.int32, (ATTN_Q_TILE, ATTN_WINDOW), 1) // CHUNK
        band = (kc >= qc) & (kc <= qc + N_PREV_CHUNKS)
        bm_ref[...] = jnp.where(band, bias, NEG_INF)

    col = lax.broadcasted_iota(jnp.int32, (ATTN_Q_TILE, ATTN_WINDOW), 1)
    for j in range(q_ref.shape[0] // ATTN_Q_TILE):
        rows = slice(j * ATTN_Q_TILE, (j + 1) * ATTN_Q_TILE)
        start = pl.multiple_of(i * q_ref.shape[0] + j * ATTN_Q_TILE, ATTN_Q_TILE)
        q = (q_ref[rows, :] * ATTN_SCALE).astype(BF16)
        kw = kbuf_ref[pl.ds(start, ATTN_WINDOW), :]
        vw = vbuf_ref[pl.ds(start, ATTN_WINDOW), :]
        s = lax.dot_general(q, kw, (((1,), (1,)), ((), ())), preferred_element_type=F32) + bm_ref[...]
        s = jnp.where(col >= BAND_ROWS - start, s, NEG_INF)
        m = jnp.max(s, axis=-1, keepdims=True)
        p = jnp.exp(s - m)
        l = jnp.sum(p, axis=-1, keepdims=True)
        o = jnp.dot(p.astype(BF16), vw, preferred_element_type=F32)
        o_ref[rows, :] = o / l


def _attn_prompt(z, frow, *, n_heads, q_blk, k_blk, v_blk, tq):
    t = z.shape[0]
    tq = min(tq, t)
    assert t % tq == 0 and tq % ATTN_Q_TILE == 0
    return pl.pallas_call(
        functools.partial(_attn_prompt_kernel, t=t),
        grid=(n_heads, t // tq),
        in_specs=[
            pl.BlockSpec((tq, HEAD_DIM), lambda h, i: (i, q_blk + h)),
            pl.BlockSpec((t, HEAD_DIM), lambda h, i: (0, k_blk + h)),
            pl.BlockSpec((t, HEAD_DIM), lambda h, i: (0, v_blk + h)),
            pl.BlockSpec((1, 1, TOEPLITZ_N), lambda h, i: (h, 0, 0)),
        ],
        out_specs=pl.BlockSpec((tq, HEAD_DIM), lambda h, i: (i, h)),
        out_shape=jax.ShapeDtypeStruct((t, n_heads * HEAD_DIM), F32),
        scratch_shapes=[
            pltpu.VMEM((BAND_ROWS + t, HEAD_DIM), BF16),
            pltpu.VMEM((BAND_ROWS + t, HEAD_DIM), BF16),
            pltpu.VMEM((ATTN_Q_TILE, ATTN_WINDOW), F32),
        ],
        compiler_params=_cparams(("arbitrary", "arbitrary")),
        name="attn_prompt",
    )(z, z, z, frow)


def _attn_sample_kernel(q_ref, k_ref, v_ref, ck_ref, cv_ref, f_ref, o_ref, bias_ref, *, n_heads, s_len, r_len):
    @pl.when(pl.program_id(0) == 0)
    def _():
        for h in range(n_heads):
            bias_ref[h] = _toeplitz(f_ref[h], s_len)

    for h in range(n_heads):
        hs = slice(h * HEAD_DIM, (h + 1) * HEAD_DIM)
        q = (q_ref[:, hs] * ATTN_SCALE).astype(BF16)
        kc = ck_ref[0, :, hs].astype(BF16)
        vc = cv_ref[0, :, hs].astype(BF16)
        kn = k_ref[:, hs].astype(BF16)
        vn = v_ref[:, hs].astype(BF16)
        nt = (((1,), (1,)), ((), ()))
        off = BAND_ROWS - r_len
        s_c = lax.dot_general(q, kc, nt, preferred_element_type=F32) + bias_ref[h, :, off:off + r_len]
        s_n = lax.dot_general(q, kn, nt, preferred_element_type=F32) + bias_ref[h, :, BAND_ROWS:BAND_ROWS + s_len]
        m = jnp.maximum(jnp.max(s_c, axis=-1, keepdims=True), jnp.max(s_n, axis=-1, keepdims=True))
        p_c = jnp.exp(s_c - m)
        p_n = jnp.exp(s_n - m)
        l = jnp.sum(p_c, axis=-1, keepdims=True) + jnp.sum(p_n, axis=-1, keepdims=True)
        o = (jnp.dot(p_c.astype(BF16), vc, preferred_element_type=F32)
             + jnp.dot(p_n.astype(BF16), vn, preferred_element_type=F32))
        o_ref[:, hs] = o / l


def _attn_sample(z, cache_k, cache_v, frow, *, n_streams, n_heads, q_blk, k_blk, v_blk):
    s_len = z.shape[0] // n_streams
    r_len = cache_k.shape[1]
    a = n_heads * HEAD_DIM
    return pl.pallas_call(
        functools.partial(_attn_sample_kernel, n_heads=n_heads, s_len=s_len, r_len=r_len),
        grid=(n_streams,),
        in_specs=[
            pl.BlockSpec((s_len, a), lambda b: (b, q_blk)),
            pl.BlockSpec((s_len, a), lambda b: (b, k_blk)),
            pl.BlockSpec((s_len, a), lambda b: (b, v_blk)),
            pl.BlockSpec((1, r_len, a), lambda b: (b, 0, 0)),
            pl.BlockSpec((1, r_len, a), lambda b: (b, 0, 0)),
            pl.BlockSpec((n_heads, 1, TOEPLITZ_N), lambda b: (0, 0, 0)),
        ],
        out_specs=pl.BlockSpec((s_len, a), lambda b: (b, 0)),
        out_shape=jax.ShapeDtypeStruct((z.shape[0], a), F32),
        scratch_shapes=[pltpu.VMEM((n_heads, s_len, TOEPLITZ_N), F32)],
        compiler_params=_cparams(("arbitrary",)),
        name="attn_sample",
    )(z, z, z, cache_k, cache_v, frow)


def _merge_kernel(cn_ref, at_ref, ga_ref, w_ref, x_ref, o_ref, an_ref):
    c = cn_ref.shape[1]

    @pl.when(pl.program_id(1) == 0)
    def _():
        _norm_rows_to(an_ref, at_ref.shape[0], lambda rows: _rms(at_ref[rows, :], ga_ref[...]))

    o_ref[...] = (x_ref[...]
                  + jnp.dot(cn_ref[...], w_ref[0:c, :], preferred_element_type=F32)
                  + jnp.dot(an_ref[...], w_ref[c:, :], preferred_element_type=F32))


def _merge(conv_n, attn, g_attn, w_out, x, *, tm, tn):
    t, d = x.shape
    c = conv_n.shape[1]
    a = attn.shape[1]
    tm = min(tm, t)
    assert t % tm == 0 and d % tn == 0 and tm % NORM_ROWS == 0
    return pl.pallas_call(
        _merge_kernel,
        grid=(t // tm, d // tn),
        in_specs=[
            _once_per_row_tile((tm, c), lambda i, j: (i, 0)),
            _once_per_row_tile((tm, a), lambda i, j: (i, 0)),
            pl.BlockSpec((1, a), lambda i, j: (0, 0)),
            pl.BlockSpec((c + a, tn), lambda i, j: (0, j)),
            pl.BlockSpec((tm, tn), lambda i, j: (i, j)),
        ],
        out_specs=pl.BlockSpec((tm, tn), lambda i, j: (i, j)),
        out_shape=jax.ShapeDtypeStruct((t, d), F32),
        scratch_shapes=[pltpu.VMEM((tm, a), BF16)],
        compiler_params=_cparams(("parallel", "arbitrary")),
        name="merge",
    )(conv_n, attn, g_attn, w_out, x)


def _top16(scores, pos, n_pos):
    lanes = pos.shape[1]
    slot = lax.broadcasted_iota(jnp.int32, (PEER_TOPK, lanes), 0)

    def body(k, carry):
        out = []
        for s, vals, picks in carry:
            m = jnp.max(s, axis=0, keepdims=True)
            idx = jnp.min(jnp.where(s == m, pos, n_pos), axis=0, keepdims=True)
            s = jnp.where(pos == idx, -jnp.inf, s)
            vals = jnp.where(slot == k, m, vals)
            picks = jnp.where(slot == k, idx, picks)
            out.append((s, vals, picks))
        return tuple(out)

    init = tuple((s, jnp.zeros((PEER_TOPK, lanes), F32), jnp.zeros((PEER_TOPK, lanes), jnp.int32)) for s in scores)
    return lax.fori_loop(0, PEER_TOPK, body, init)


_GRID_HALF = PEER_TOPK // 2
SELECT_GROUP = 2


def _candidate_grid(v1, v2, iota8):
    neg = -jnp.inf
    rows = [v1[0:1, :] + v2]
    for a in range(1, _GRID_HALF):
        rows.append(jnp.where(iota8 < PEER_TOPK // (a + 1), v1[a:a + 1, :] + v2[0:_GRID_HALF, :], neg))
    rows.append(v1[_GRID_HALF:, :] + v2[0:1, :])
    return jnp.concatenate(rows, axis=0)


def _candidate_pos(iota8):
    pos = [iota8, iota8 + _GRID_HALF]
    pos += [iota8 + a * PEER_TOPK for a in range(1, _GRID_HALF)]
    pos.append((iota8 + _GRID_HALF) * PEER_TOPK)
    return jnp.concatenate(pos, axis=0)


def _kept_counts(left, iota8):
    gone = (left == -jnp.inf).astype(F32)
    cnt = [jnp.sum(gone[0:PEER_TOPK, :], axis=0, keepdims=True)]
    for a in range(1, _GRID_HALF):
        r0 = PEER_TOPK + (a - 1) * _GRID_HALF
        piece = jnp.where(iota8 < PEER_TOPK // (a + 1), gone[r0:r0 + _GRID_HALF, :], 0.0)
        cnt.append(jnp.sum(piece, axis=0, keepdims=True))
    r0 = PEER_TOPK + (_GRID_HALF - 1) * _GRID_HALF
    cnt += [gone[r0 + i:r0 + i + 1, :] for i in range(_GRID_HALF)]
    return cnt


def _peer_select_kernel(q_ref, keys_ref, cnt_ref, e1_ref, r2_ref, e2_ref, *, group):
    lanes = q_ref.shape[0]
    key_row = lax.broadcasted_iota(jnp.int32, (N_KEYS, lanes), 0)
    iota8 = lax.broadcasted_iota(jnp.int32, (_GRID_HALF, lanes), 0)
    grid_pos = _candidate_pos(iota8)
    nt = (((1,), (1,)), ((), ()))
    for h0 in range(0, PEER_HEADS, group):
        heads = range(h0, h0 + group)
        scores = []
        for h in heads:
            for p, e_ref in ((0, e1_ref), (1, e2_ref)):
                c0 = (h * 2 + p) * PEER_HALF
                qhp = q_ref[:, c0:c0 + PEER_HALF].astype(BF16)
                s = lax.dot_general(keys_ref[h, p], qhp, nt, preferred_element_type=F32)
                e_ref[h] = jnp.exp(s - jnp.max(s, axis=0, keepdims=True))
                scores.append(s)
        found = _top16(scores, key_row, N_KEYS)
        grids = []
        for i, h in enumerate(heads):
            (_, v1, _), (_, v2, picks2) = found[2 * i], found[2 * i + 1]
            r2 = jnp.full((N_KEYS, lanes), float(PEER_TOPK), F32)
            for k in range(PEER_TOPK):
                r2 = jnp.where(key_row == picks2[k:k + 1, :], float(k), r2)
            r2_ref[h] = r2
            grids.append(_candidate_grid(v1, v2, iota8))
        best = _top16(grids, grid_pos, PEER_TOPK * PEER_TOPK)
        for i, h in enumerate(heads):
            left, vals, _ = best[i]
            z = jnp.sum(jnp.exp(vals - vals[0:1, :]), axis=0, keepdims=True)
            cnt_a = _kept_counts(left, iota8)
            picks1 = found[2 * i][2]
            cnt = jnp.zeros((N_KEYS, lanes), F32)
            for k in range(PEER_TOPK):
                cnt = jnp.where(key_row == picks1[k:k + 1, :], cnt_a[k], cnt)
            cnt_ref[h] = cnt
            e1_ref[h] = e1_ref[h] * (1.0 / z)


def _peer_select(q, sub_keys, *, tq):
    t = q.shape[0]
    tq = min(tq, t)
    tab = jax.ShapeDtypeStruct((PEER_HEADS, N_KEYS, t), F32)
    tab_spec = pl.BlockSpec((PEER_HEADS, N_KEYS, tq), lambda i: (0, 0, i))
    return pl.pallas_call(
        functools.partial(_peer_select_kernel, group=SELECT_GROUP),
        grid=(t // tq,),
        in_specs=[
            pl.BlockSpec((tq, q.shape[1]), lambda i: (i, 0)),
            pl.BlockSpec(sub_keys.shape, lambda i: (0, 0, 0, 0)),
        ],
        out_specs=[tab_spec] * 4,
        out_shape=[tab] * 4,
        compiler_params=_cparams(("parallel",)),
        name="peer_select",
    )(q, sub_keys)


def _peer_ffn_kernel(hn_ref, u_ref, v_ref, cnt_ref, e1_ref, r2_ref, e2_ref, o_ref, act_ref, a_ref,
                     *, rows_per_tile, n_exp_tiles):
    s = pl.program_id(0)
    jp = jnp.maximum(s - 1, 0) % n_exp_tiles

    @pl.when(s == 0)
    def _():
        act_ref[...] = jnp.zeros_like(act_ref)

    @pl.when(jp == 0)
    def _():
        o_ref[...] = jnp.zeros_like(o_ref)

    tm = act_ref.shape[1]

    def finish_rows(r):
        row = (jp * rows_per_tile) % SUBLANES + r
        gate = jnp.zeros((N_KEYS, tm), F32)
        for h in range(PEER_HEADS):
            cnt = cnt_ref[h, pl.ds(row, 1), :]
            e1 = e1_ref[h, pl.ds(row, 1), :]
            gate = gate + jnp.where(r2_ref[h] < cnt, e2_ref[h], 0.0) * e1
        x = act_ref[r * N_KEYS:(r + 1) * N_KEYS, :]
        gelu = 0.5 * x * (1.0 + lax.erf(x * (2.0 ** -0.5)))
        a_ref[r * N_KEYS:(r + 1) * N_KEYS, :] = (gate * gelu).astype(BF16)

    nt = (((1,), (1,)), ((), ()))
    act_new = lax.dot_general(u_ref[...], hn_ref[...], nt, preferred_element_type=F32)
    for r in range(rows_per_tile):
        finish_rows(r)
    tn = (((0,), (0,)), ((), ()))
    o_ref[...] += lax.dot_general(a_ref[...], v_ref[...], tn, preferred_element_type=F32)
    act_ref[...] = act_new


def _peer_ffn(hn, u_bf, v_bf, tables, *, tm, te):
    t, d = hn.shape
    n_exp = u_bf.shape[0]
    tm = min(tm, t)
    rows_per_tile = te // N_KEYS
    assert SUBLANES % rows_per_tile == 0 and t % tm == 0 and n_exp % te == 0
    ni, nj = t // tm, n_exp // te
    last = ni * nj - 1

    def cur(s):
        s = jnp.minimum(s, last)
        return s // nj, s % nj

    def prev(s):
        s = jnp.maximum(s - 1, 0)
        return s // nj, s % nj

    tab_spec = pl.BlockSpec((PEER_HEADS, N_KEYS, tm), lambda s: (0, 0, prev(s)[0]))
    row_spec = pl.BlockSpec((PEER_HEADS, SUBLANES, tm),
                            lambda s: (0, (prev(s)[1] * rows_per_tile) // SUBLANES, prev(s)[0]))
    return pl.pallas_call(
        functools.partial(_peer_ffn_kernel, rows_per_tile=rows_per_tile, n_exp_tiles=nj),
        grid=(ni * nj + 1,),
        in_specs=[
            pl.BlockSpec((tm, d), lambda s: (cur(s)[0], 0)),
            pl.BlockSpec((te, d), lambda s: (cur(s)[1], 0)),
            pl.BlockSpec((te, d), lambda s: (prev(s)[1], 0)),
            row_spec, row_spec, tab_spec, tab_spec,
        ],
        out_specs=pl.BlockSpec((tm, d), lambda s: (prev(s)[0], 0)),
        out_shape=jax.ShapeDtypeStruct((t, d), F32),
        scratch_shapes=[pltpu.VMEM((te, tm), F32), pltpu.VMEM((te, tm), BF16)],
        compiler_params=_cparams(("arbitrary",)),
        name="peer_ffn",
    )(hn, u_bf, v_bf, *tables)


def _final_kernel(x_ref, p_ref, g_ref, o_ref):
    o_ref[...] = _rms(x_ref[...] + p_ref[...], g_ref[...])


def _final(x1, peer, g, *, tm):
    t, d = x1.shape
    tm = min(tm, t)
    spec = pl.BlockSpec((tm, d), lambda i: (i, 0))
    return pl.pallas_call(
        _final_kernel,
        grid=(t // tm,),
        in_specs=[spec, spec, pl.BlockSpec((1, d), lambda i: (0, 0))],
        out_specs=spec,
        out_shape=jax.ShapeDtypeStruct((t, d), F32),
        compiler_params=_cparams(("parallel",)),
        name="final_norm",
    )(x1, peer, g)


def _group(x, hist, cache_kv, p, *, n_streams):
    c = p["w_dw"].shape[1]
    a = p["g_attn"].shape[1]
    n_heads = a // HEAD_DIM
    tl = TILES
    z = _norm_matmul(x, p["mix_g"], p["w_in"], p["b_in"], tm=tl["proj_rows"], tn=tl["proj_cols"])
    u, conv_n = _conv_module(z, hist, p["w_dw"], p["b_dw"], p["ln_g"], p["ln_b"], p["g_conv"],
                             n_streams=n_streams, tt=tl["conv_rows"])
    if cache_kv is None:
        blk = 2 * c // HEAD_DIM
        attn = _attn_prompt(z, p["frow"], n_heads=n_heads, q_blk=blk, k_blk=blk + n_heads, v_blk=blk + 2 * n_heads,
                            tq=tl["attn_rows"])
    else:
        blk = 2 * c // a
        attn = _attn_sample(z, cache_kv[0], cache_kv[1], p["frow"], n_streams=n_streams, n_heads=n_heads,
                            q_blk=blk, k_blk=blk + 1, v_blk=blk + 2)
    x1 = _merge(conv_n, attn, p["g_attn"], p["w_out"], x, tm=tl["proj_rows"], tn=tl["proj_cols"])
    q, hn = _norm_matmul(x1, p["ffn_g"], p["w_q"], p["zero_bq"], tm=tl["proj_rows"], tn=tl["proj_cols"],
                         emit_norm=True)
    tables = _peer_select(q, p["sub_keys"], tq=tl["select_tokens"])
    peer = _peer_ffn(hn, p["peer_u"], p["peer_v"], tables, tm=tl["ffn_tokens"], te=tl["ffn_experts"])
    y = _final(x1, peer, p["final_g"], tm=tl["final_rows"])
    return y, z, u


def kernel(x_prompt, x_sample, cache_conv, cache_k, cache_v, mix_norm_g, w_in, b_in, w_dw, b_dw, ln_g, ln_b,
           rel_bias, out_norm_conv_g, out_norm_attn_g, w_out, ffn_norm_g, peer_w_q, peer_sub_keys, peer_u, peer_v,
           final_norm_g):
    assert mix_norm_g.shape[0] == 1, "single-layer step"
    bsz, seq, d = x_prompt.shape
    db, ds, _ = x_sample.shape
    c = w_dw.shape[2]
    a = out_norm_attn_g.shape[1]
    n_heads = a // HEAD_DIM
    assert bsz == 1 and seq % ATTN_Q_TILE == 0 and ds >= CONV_WIDTH - 1

    row = lambda v: v.reshape(1, -1)
    p = dict(
        mix_g=row(mix_norm_g[0]), w_in=w_in[0].astype(BF16), b_in=row(b_in[0]),
        w_dw=w_dw[0], b_dw=row(b_dw[0]), ln_g=row(ln_g[0]), ln_b=row(ln_b[0]),
        g_conv=row(out_norm_conv_g[0]), g_attn=row(out_norm_attn_g[0]), w_out=w_out[0].astype(BF16),
        ffn_g=row(ffn_norm_g[0]), w_q=peer_w_q[0].astype(BF16),
        zero_bq=jnp.zeros((1, peer_w_q.shape[2]), F32),
        sub_keys=peer_sub_keys[0].astype(BF16), peer_u=peer_u[0].astype(BF16), peer_v=peer_v[0].astype(BF16),
        final_g=row(final_norm_g),
        frow=rel_bias[0][:, _bias_row_index()].reshape(n_heads, 1, TOEPLITZ_N),
    )

    hist_p = jnp.zeros((1, CONV_HALO, c), F32)
    y_p, z_p, u_p = _group(x_prompt.reshape(seq, d), hist_p, None, p, n_streams=1)

    pad = CONV_HALO - (CONV_WIDTH - 1)
    hist_s = jnp.pad(cache_conv[0], ((0, 0), (pad, 0), (0, 0)))
    r_len = cache_k.shape[2]
    cache_kv = (cache_k[0].reshape(db, r_len, a), cache_v[0].reshape(db, r_len, a))
    y_s, z_s, u_s = _group(x_sample.reshape(db * ds, d), hist_s, cache_kv, p, n_streams=db)

    k0, v0 = 2 * c + a, 2 * c + 2 * a
    rows_p = min(BAND_ROWS, seq)
    keep = CONV_WIDTH - 1
    return (
        y_p.reshape(1, seq, d),
        y_s.reshape(db, ds, d),
        u_p[seq - keep:].reshape(1, 1, keep, c),
        z_p[seq - rows_p:, k0:k0 + a].reshape(1, 1, rows_p, n_heads, HEAD_DIM),
        z_p[seq - rows_p:, v0:v0 + a].reshape(1, 1, rows_p, n_heads, HEAD_DIM),
        u_s.reshape(db, ds, c)[:, ds - keep:].reshape(1, db, keep, c),
        z_s[:, k0:k0 + a].reshape(1, db, ds, n_heads, HEAD_DIM),
        z_s[:, v0:v0 + a].reshape(1, db, ds, n_heads, HEAD_DIM),
    )
```

```python
import functools

import numpy as np
import jax
import jax.numpy as jnp
from jax import lax
from jax.experimental import pallas as pl
from jax.experimental.pallas import tpu as pltpu

F32 = jnp.float32
BF16 = jnp.bfloat16

CHUNK = 64
CONV_WIDTH = 31
CONV_HALO = 32
HEAD_DIM = 128
N_PREV_CHUNKS = 8
BAND_ROWS = N_PREV_CHUNKS * CHUNK
REL_CLIP = 256
ATTN_SCALE = HEAD_DIM ** -0.5
PEER_HEADS = 8
PEER_HALF = 128
N_KEYS = 128
PEER_TOPK = 16
SUBLANES = 8
MXU_WIDTH = 256
EPS = 1e-6
NEG_INF = -1e30

ATTN_Q_TILE = 2 * CHUNK
ATTN_WINDOW = BAND_ROWS + ATTN_Q_TILE
TOEPLITZ_N = 768

VMEM_LIMIT = 56 * 1024 * 1024

TILES = dict(
    in_proj_rows=1024, proj_rows=512, proj_cols=512,
    conv_rows=128,
    attn_rows=1024,
    select_tokens=128,
    ffn_tokens=512, ffn_experts=512,
)


def _cparams(sem, flags=None):
    return pltpu.CompilerParams(dimension_semantics=sem, vmem_limit_bytes=VMEM_LIMIT, flags=flags)


def _rms(x, g):
    ms = jnp.mean(x * x, axis=-1, keepdims=True)
    return x * lax.rsqrt(ms + EPS) * g


NORM_ROWS = 16


def _norm_rows_to(dst_ref, n_rows, row_fn):
    def step(r, carry):
        rows = pl.ds(pl.multiple_of(r * NORM_ROWS, NORM_ROWS), NORM_ROWS)
        dst_ref[rows, :] = row_fn(rows).astype(BF16)
        return carry

    lax.fori_loop(0, n_rows // NORM_ROWS, step, 0)


def _norm_matmul_kernel(x_ref, g_ref, w_ref, b_ref, o_ref, xn_ref):
    @pl.when(pl.program_id(1) == 0)
    def _():
        _norm_rows_to(xn_ref, x_ref.shape[0], lambda rows: _rms(x_ref[rows, :], g_ref[...]))

    o_ref[...] = jnp.dot(xn_ref[...], w_ref[...].astype(BF16), preferred_element_type=F32) + b_ref[...]


def _row_tile_spec(block_shape, index_map, single_buffer):
    if single_buffer:
        return pl.BlockSpec(block_shape, index_map, pipeline_mode=pl.Buffered(1))
    return pl.BlockSpec(block_shape, index_map)


def _norm_matmul(x, g, w, b, *, tm, tn, emit_norm=False, single_buffer=False):
    t, d = x.shape
    n = w.shape[1]
    tm = min(tm, t)
    assert t % tm == 0 and n % tn == 0 and tm % NORM_ROWS == 0
    out_shape = [jax.ShapeDtypeStruct((t, n), F32)]
    out_specs = [pl.BlockSpec((tm, tn), lambda i, j: (i, j))]
    scratch = [pltpu.VMEM((tm, d), BF16)]
    if emit_norm:
        out_shape.append(jax.ShapeDtypeStruct((t, d), BF16))
        out_specs.append(pl.BlockSpec((tm, d), lambda i, j: (i, 0)))
        scratch = []
    res = pl.pallas_call(
        _norm_matmul_kernel,
        grid=(t // tm, n // tn),
        in_specs=[
            _row_tile_spec((tm, d), lambda i, j: (i, 0), single_buffer),
            pl.BlockSpec((1, d), lambda i, j: (0, 0)),
            pl.BlockSpec((d, tn), lambda i, j: (0, j)),
            pl.BlockSpec((1, tn), lambda i, j: (0, j)),
        ],
        out_specs=out_specs,
        out_shape=out_shape,
        scratch_shapes=scratch,
        compiler_params=_cparams(("parallel", "arbitrary")),
        name="norm_matmul_emit" if emit_norm else "norm_matmul",
    )(x, g, w, b)
    return res if emit_norm else res[0]


def _conv_kernel(val_ref, gate_ref, hist_ref, w_ref, bdw_ref, lng_ref, lnb_ref, gc_ref,
                 u_ref, o_ref, ush_ref, par_ref, *, tt):
    t = pl.program_id(1)
    n_rows = CONV_HALO + tt
    n_ch = w_ref.shape[1]

    @pl.when((pl.program_id(0) == 0) & (t == 0))
    def _():
        for k in range(CONV_WIDTH):
            par_ref[k] = jnp.broadcast_to(w_ref[k:k + 1, :], (SUBLANES, n_ch))
        for k, ref in enumerate((bdw_ref, lng_ref, lnb_ref, gc_ref)):
            par_ref[CONV_WIDTH + k] = jnp.broadcast_to(ref[...], (SUBLANES, n_ch))

    @pl.when(t == 0)
    def _():
        ush_ref[0, 0:CONV_HALO, :] = hist_ref[0]

    @pl.when(t > 0)
    def _():
        ush_ref[0, 0:CONV_HALO, :] = ush_ref[0, tt:tt + CONV_HALO, :]

    u = val_ref[...] * jax.nn.sigmoid(gate_ref[...])
    u_ref[...] = u
    ush_ref[0, CONV_HALO:n_rows, :] = u
    upad = ush_ref[0]
    for s in range(1, SUBLANES):
        ush_ref[s] = pltpu.roll(upad, n_rows - s, 0)

    rows = SUBLANES
    for c in range(tt // rows):
        r0 = c * rows
        acc = par_ref[CONV_WIDTH]
        for k in range(CONV_WIDTH):
            off = CONV_HALO - (CONV_WIDTH - 1) + k
            lo = r0 + off - off % SUBLANES
            acc = acc + ush_ref[off % SUBLANES, lo:lo + rows, :] * par_ref[k]
        mu = jnp.mean(acc, axis=-1, keepdims=True)
        cen = acc - mu
        var = jnp.mean(cen * cen, axis=-1, keepdims=True)
        yn = cen * lax.rsqrt(var + EPS) * par_ref[CONV_WIDTH + 1] + par_ref[CONV_WIDTH + 2]
        s = yn * jax.nn.sigmoid(yn)
        o_ref[r0:r0 + rows, :] = _rms(s, par_ref[CONV_WIDTH + 3]).astype(BF16)


def _conv_module(z, hist, w_dw, b_dw, ln_g, ln_b, g_conv, *, n_streams, tt):
    t_all = z.shape[0]
    c = w_dw.shape[1]
    t_stream = t_all // n_streams
    tt = min(tt, t_stream)
    nt = t_stream // tt
    return pl.pallas_call(
        functools.partial(_conv_kernel, tt=tt),
        grid=(n_streams, nt),
        in_specs=[
            pl.BlockSpec((tt, c), lambda b, t: (b * nt + t, 0)),
            pl.BlockSpec((tt, c), lambda b, t: (b * nt + t, 1)),
            pl.BlockSpec((1, CONV_HALO, c), lambda b, t: (b, 0, 0)),
            pl.BlockSpec((CONV_WIDTH, c), lambda b, t: (0, 0)),
            pl.BlockSpec((1, c), lambda b, t: (0, 0)),
            pl.BlockSpec((1, c), lambda b, t: (0, 0)),
            pl.BlockSpec((1, c), lambda b, t: (0, 0)),
            pl.BlockSpec((1, c), lambda b, t: (0, 0)),
        ],
        out_specs=[
            pl.BlockSpec((tt, c), lambda b, t: (b * nt + t, 0)),
            pl.BlockSpec((tt, c), lambda b, t: (b * nt + t, 0)),
        ],
        out_shape=[jax.ShapeDtypeStruct((t_all, c), F32), jax.ShapeDtypeStruct((t_all, c), BF16)],
        scratch_shapes=[pltpu.VMEM((SUBLANES, CONV_HALO + tt, c), F32),
                        pltpu.VMEM((CONV_WIDTH + 4, SUBLANES, c), F32)],
        compiler_params=_cparams(("arbitrary", "arbitrary")),
        name="conv_module",
    )(z, z, hist, w_dw, b_dw, ln_g, ln_b, g_conv)


def _bias_row_index():
    m = np.arange(TOEPLITZ_N)
    dist = np.where(m <= TOEPLITZ_N - ATTN_Q_TILE, BAND_ROWS - m, BAND_ROWS + TOEPLITZ_N - m)
    return np.clip(dist, -REL_CLIP, REL_CLIP) + REL_CLIP


def _toeplitz(frow, rows):
    x = jnp.broadcast_to(frow, (rows, frow.shape[-1]))
    return pltpu.roll(x, 0, 1, stride=1, stride_axis=0)


def _attn_prompt_kernel(q_ref, k_ref, v_ref, f_ref, o_ref, kbuf_ref, vbuf_ref, bm_ref, *, t):
    i = pl.program_id(1)

    @pl.when(i == 0)
    def _():
        zeros = jnp.zeros((BAND_ROWS, HEAD_DIM), BF16)
        kbuf_ref[0:BAND_ROWS, :] = zeros
        vbuf_ref[0:BAND_ROWS, :] = zeros
        kbuf_ref[BAND_ROWS:BAND_ROWS + t, :] = k_ref[...].astype(BF16)
        vbuf_ref[BAND_ROWS:BAND_ROWS + t, :] = v_ref[...].astype(BF16)
        bias = _toeplitz(f_ref[0], ATTN_Q_TILE)[:, :ATTN_WINDOW]
        qc = lax.broadcasted_iota(jnp.int32, (ATTN_Q_TILE, ATTN_WINDOW), 0) // CHUNK
        kc = lax.broadcasted_iota(jnp.int32, (ATTN_Q_TILE, ATTN_WINDOW), 1) // CHUNK
        band = (kc >= qc) & (kc <= qc + N_PREV_CHUNKS)
        bm_ref[...] = jnp.where(band, bias, NEG_INF)

    col = lax.broadcasted_iota(jnp.int32, (ATTN_Q_TILE, ATTN_WINDOW), 1)
    for j in range(q_ref.shape[0] // ATTN_Q_TILE):
        rows = slice(j * ATTN_Q_TILE, (j + 1) * ATTN_Q_TILE)
        start = pl.multiple_of(i * q_ref.shape[0] + j * ATTN_Q_TILE, ATTN_Q_TILE)
        q = (q_ref[rows, :] * ATTN_SCALE).astype(BF16)
        kw = kbuf_ref[pl.ds(start, ATTN_WINDOW), :]
        vw = vbuf_ref[pl.ds(start, ATTN_WINDOW), :]
        s = lax.dot_general(q, kw, (((1,), (1,)), ((), ())), preferred_element_type=F32) + bm_ref[...]
        s = jnp.where(col >= BAND_ROWS - start, s, NEG_INF)
        m = jnp.max(s, axis=-1, keepdims=True)
        p = jnp.exp(s - m)
        l = jnp.sum(p, axis=-1, keepdims=True)
        o = jnp.dot(p.astype(BF16), vw, preferred_element_type=F32)
        o_ref[rows, :] = o / l


def _attn_prompt(z, frow, *, n_heads, q_blk, k_blk, v_blk, tq):
    t = z.shape[0]
    tq = min(tq, t)
    assert t % tq == 0 and tq % ATTN_Q_TILE == 0
    return pl.pallas_call(
        functools.partial(_attn_prompt_kernel, t=t),
        grid=(n_heads, t // tq),
        in_specs=[
            pl.BlockSpec((tq, HEAD_DIM), lambda h, i: (i, q_blk + h)),
            pl.BlockSpec((t, HEAD_DIM), lambda h, i: (0, k_blk + h)),
            pl.BlockSpec((t, HEAD_DIM), lambda h, i: (0, v_blk + h)),
            pl.BlockSpec((1, 1, TOEPLITZ_N), lambda h, i: (h, 0, 0)),
        ],
        out_specs=pl.BlockSpec((tq, HEAD_DIM), lambda h, i: (i, h)),
        out_shape=jax.ShapeDtypeStruct((t, n_heads * HEAD_DIM), F32),
        scratch_shapes=[
            pltpu.VMEM((BAND_ROWS + t, HEAD_DIM), BF16),
            pltpu.VMEM((BAND_ROWS + t, HEAD_DIM), BF16),
            pltpu.VMEM((ATTN_Q_TILE, ATTN_WINDOW), F32),
        ],
        compiler_params=_cparams(("arbitrary", "arbitrary")),
        name="attn_prompt",
    )(z, z, z, frow)


def _attn_sample_kernel(q_ref, k_ref, v_ref, ck_ref, cv_ref, f_ref, o_ref, bias_ref, *, n_heads, s_len, r_len):
    @pl.when(pl.program_id(0) == 0)
    def _():
        for h in range(n_heads):
            bias_ref[h] = _toeplitz(f_ref[h], s_len)

    for h in range(n_heads):
        hs = slice(h * HEAD_DIM, (h + 1) * HEAD_DIM)
        q = (q_ref[:, hs] * ATTN_SCALE).astype(BF16)
        kc = ck_ref[0, :, hs].astype(BF16)
        vc = cv_ref[0, :, hs].astype(BF16)
        kn = k_ref[:, hs].astype(BF16)
        vn = v_ref[:, hs].astype(BF16)
        nt = (((1,), (1,)), ((), ()))
        off = BAND_ROWS - r_len
        s_c = lax.dot_general(q, kc, nt, preferred_element_type=F32) + bias_ref[h, :, off:off + r_len]
        s_n = lax.dot_general(q, kn, nt, preferred_element_type=F32) + bias_ref[h, :, BAND_ROWS:BAND_ROWS + s_len]
        m = jnp.maximum(jnp.max(s_c, axis=-1, keepdims=True), jnp.max(s_n, axis=-1, keepdims=True))
        p_c = jnp.exp(s_c - m)
        p_n = jnp.exp(s_n - m)
        l = jnp.sum(p_c, axis=-1, keepdims=True) + jnp.sum(p_n, axis=-1, keepdims=True)
        o = (jnp.dot(p_c.astype(BF16), vc, preferred_element_type=F32)
             + jnp.dot(p_n.astype(BF16), vn, preferred_element_type=F32))
        o_ref[:, hs] = o / l


def _attn_sample(z, cache_k, cache_v, frow, *, n_streams, n_heads, q_blk, k_blk, v_blk):
    s_len = z.shape[0] // n_streams
    r_len = cache_k.shape[1]
    a = n_heads * HEAD_DIM
    return pl.pallas_call(
        functools.partial(_attn_sample_kernel, n_heads=n_heads, s_len=s_len, r_len=r_len),
        grid=(n_streams,),
        in_specs=[
            pl.BlockSpec((s_len, a), lambda b: (b, q_blk)),
            pl.BlockSpec((s_len, a), lambda b: (b, k_blk)),
            pl.BlockSpec((s_len, a), lambda b: (b, v_blk)),
            pl.BlockSpec((1, r_len, a), lambda b: (b, 0, 0)),
            pl.BlockSpec((1, r_len, a), lambda b: (b, 0, 0)),
            pl.BlockSpec((n_heads, 1, TOEPLITZ_N), lambda b: (0, 0, 0)),
        ],
        out_specs=pl.BlockSpec((s_len, a), lambda b: (b, 0)),
        out_shape=jax.ShapeDtypeStruct((z.shape[0], a), F32),
        scratch_shapes=[pltpu.VMEM((n_heads, s_len, TOEPLITZ_N), F32)],
        compiler_params=_cparams(("arbitrary",)),
        name="attn_sample",
    )(z, z, z, cache_k, cache_v, frow)


def _merge_kernel(cn_ref, at_ref, ga_ref, w_ref, x_ref, o_ref, an_ref):
    c = cn_ref.shape[1]

    @pl.when(pl.program_id(1) == 0)
    def _():
        _norm_rows_to(an_ref, at_ref.shape[0], lambda rows: _rms(at_ref[rows, :], ga_ref[...]))

    o_ref[...] = (x_ref[...]
                  + jnp.dot(cn_ref[...], w_ref[0:c, :].astype(BF16), preferred_element_type=F32)
                  + jnp.dot(an_ref[...], w_ref[c:, :].astype(BF16), preferred_element_type=F32))


def _merge(conv_n, attn, g_attn, w_out, x, *, tm, tn):
    t, d = x.shape
    c = conv_n.shape[1]
    a = attn.shape[1]
    tm = min(tm, t)
    assert t % tm == 0 and d % tn == 0 and tm % NORM_ROWS == 0
    return pl.pallas_call(
        _merge_kernel,
        grid=(t // tm, d // tn),
        in_specs=[
            pl.BlockSpec((tm, c), lambda i, j: (i, 0)),
            pl.BlockSpec((tm, a), lambda i, j: (i, 0)),
            pl.BlockSpec((1, a), lambda i, j: (0, 0)),
            pl.BlockSpec((c + a, tn), lambda i, j: (0, j)),
            pl.BlockSpec((tm, tn), lambda i, j: (i, j)),
        ],
        out_specs=pl.BlockSpec((tm, tn), lambda i, j: (i, j)),
        out_shape=jax.ShapeDtypeStruct((t, d), F32),
        scratch_shapes=[pltpu.VMEM((tm, a), BF16)],
        compiler_params=_cparams(("parallel", "arbitrary")),
        name="merge",
    )(conv_n, attn, g_attn, w_out, x)


def _top16(scores, pos, n_pos):
    lanes = pos.shape[1]
    slot = lax.broadcasted_iota(jnp.int32, (PEER_TOPK, lanes), 0)

    def body(k, carry):
        out = []
        for s, vals, picks in carry:
            m = jnp.max(s, axis=0, keepdims=True)
            idx = jnp.min(jnp.where(s == m, pos, n_pos), axis=0, keepdims=True)
            s = jnp.where(pos == idx, -jnp.inf, s)
            vals = jnp.where(slot == k, m, vals)
            picks = jnp.where(slot == k, idx, picks)
            out.append((s, vals, picks))
        return tuple(out)

    init = tuple((s, jnp.zeros((PEER_TOPK, lanes), F32), jnp.zeros((PEER_TOPK, lanes), jnp.int32)) for s in scores)
    return lax.fori_loop(0, PEER_TOPK, body, init)


_GRID_HALF = PEER_TOPK // 2
SELECT_GROUP = 2


def _candidate_grid(v1, v2, iota8):
    neg = -jnp.inf
    rows = [v1[0:1, :] + v2]
    for a in range(1, _GRID_HALF):
        rows.append(jnp.where(iota8 < PEER_TOPK // (a + 1), v1[a:a + 1, :] + v2[0:_GRID_HALF, :], neg))
    rows.append(v1[_GRID_HALF:, :] + v2[0:1, :])
    return jnp.concatenate(rows, axis=0)


def _candidate_pos(iota8):
    pos = [iota8, iota8 + _GRID_HALF]
    pos += [iota8 + a * PEER_TOPK for a in range(1, _GRID_HALF)]
    pos.append((iota8 + _GRID_HALF) * PEER_TOPK)
    return jnp.concatenate(pos, axis=0)


def _kept_counts(left, iota8):
    gone = (left == -jnp.inf).astype(F32)
    cnt = [jnp.sum(gone[0:PEER_TOPK, :], axis=0, keepdims=True)]
    for a in range(1, _GRID_HALF):
        r0 = PEER_TOPK + (a - 1) * _GRID_HALF
        piece = jnp.where(iota8 < PEER_TOPK // (a + 1), gone[r0:r0 + _GRID_HALF, :], 0.0)
        cnt.append(jnp.sum(piece, axis=0, keepdims=True))
    r0 = PEER_TOPK + (_GRID_HALF - 1) * _GRID_HALF
    cnt += [gone[r0 + i:r0 + i + 1, :] for i in range(_GRID_HALF)]
    return cnt


def _peer_select_kernel(q_ref, keys_ref, cnt_ref, e1_ref, r2_ref, e2_ref, *, group):
    lanes = q_ref.shape[0]
    key_row = lax.broadcasted_iota(jnp.int32, (N_KEYS, lanes), 0)
    iota8 = lax.broadcasted_iota(jnp.int32, (_GRID_HALF, lanes), 0)
    grid_pos = _candidate_pos(iota8)
    nt = (((1,), (1,)), ((), ()))
    for h0 in range(0, PEER_HEADS, group):
        heads = range(h0, h0 + group)
        scores = []
        for h in heads:
            for p, e_ref in ((0, e1_ref), (1, e2_ref)):
                c0 = (h * 2 + p) * PEER_HALF
                qhp = q_ref[:, c0:c0 + PEER_HALF].astype(BF16)
                s = lax.dot_general(keys_ref[h, p], qhp, nt, preferred_element_type=F32)
                e_ref[h] = jnp.exp(s - jnp.max(s, axis=0, keepdims=True)).astype(e_ref.dtype)
                scores.append(s)
        found = _top16(scores, key_row, N_KEYS)
        grids = []
        for i, h in enumerate(heads):
            (_, v1, _), (_, v2, picks2) = found[2 * i], found[2 * i + 1]
            r2 = jnp.full((N_KEYS, lanes), float(PEER_TOPK), F32)
            for k in range(PEER_TOPK):
                r2 = jnp.where(key_row == picks2[k:k + 1, :], float(k), r2)
            r2_ref[h] = r2.astype(r2_ref.dtype)
            grids.append(_candidate_grid(v1, v2, iota8))
        best = _top16(grids, grid_pos, PEER_TOPK * PEER_TOPK)
        for i, h in enumerate(heads):
            left, vals, _ = best[i]
            z = jnp.sum(jnp.exp(vals - vals[0:1, :]), axis=0, keepdims=True)
            cnt_a = _kept_counts(left, iota8)
            picks1 = found[2 * i][2]
            cnt = jnp.zeros((N_KEYS, lanes), F32)
            for k in range(PEER_TOPK):
                cnt = jnp.where(key_row == picks1[k:k + 1, :], cnt_a[k], cnt)
            cnt_ref[h] = cnt
            e1_ref[h] = e1_ref[h] * (1.0 / z)


def _peer_select(q, sub_keys, *, tq):
    t = q.shape[0]
    tq = min(tq, t)
    tab = jax.ShapeDtypeStruct((PEER_HEADS, N_KEYS, t), F32)
    tab_bf = jax.ShapeDtypeStruct((PEER_HEADS, N_KEYS, t), BF16)
    tab_spec = pl.BlockSpec((PEER_HEADS, N_KEYS, tq), lambda i: (0, 0, i))
    return pl.pallas_call(
        functools.partial(_peer_select_kernel, group=SELECT_GROUP),
        grid=(t // tq,),
        in_specs=[
            pl.BlockSpec((tq, q.shape[1]), lambda i: (i, 0)),
            pl.BlockSpec(sub_keys.shape, lambda i: (0, 0, 0, 0)),
        ],
        out_specs=[tab_spec] * 4,
        out_shape=[tab, tab, tab_bf, tab_bf],
        compiler_params=_cparams(("parallel",)),
        name="peer_select",
    )(q, sub_keys)


def _peer_ffn_kernel(hn_ref, u_ref, v_ref, cnt_ref, e1_ref, r2_ref, e2_ref, x1_ref, fg_ref, o_ref, act_ref, a_ref,
                     *, rows_per_tile, n_exp_tiles):
    s = pl.program_id(0)
    jp = jnp.maximum(s - 1, 0) % n_exp_tiles

    @pl.when(s == 0)
    def _():
        act_ref[...] = jnp.zeros_like(act_ref)

    @pl.when(jp == 0)
    def _():
        o_ref[...] = jnp.zeros_like(o_ref)

    tm = act_ref.shape[1]

    def finish_rows(r):
        row = (jp * rows_per_tile) % SUBLANES + r
        gate = jnp.zeros((N_KEYS, tm), BF16)
        for h in range(PEER_HEADS):
            cnt = cnt_ref[h, pl.ds(row, 1), :].astype(BF16)
            e1 = e1_ref[h, pl.ds(row, 1), :].astype(BF16)
            gate = gate + jnp.where(r2_ref[h] < cnt, e2_ref[h], jnp.zeros((), BF16)) * e1
        x = act_ref[r * N_KEYS:(r + 1) * N_KEYS, :]
        gelu = 0.5 * x * (1.0 + lax.erf(x * (2.0 ** -0.5)))
        a_ref[r * N_KEYS:(r + 1) * N_KEYS, :] = (gate.astype(F32) * gelu).astype(BF16)

    nt = (((1,), (1,)), ((), ()))
    act_new = lax.dot_general(u_ref[...], hn_ref[...], nt, preferred_element_type=F32)
    for r in range(rows_per_tile):
        finish_rows(r)
    tn = (((0,), (0,)), ((), ()))
    o_ref[...] += lax.dot_general(a_ref[...], v_ref[...], tn, preferred_element_type=F32)
    act_ref[...] = act_new

    @pl.when(jp == n_exp_tiles - 1)
    def _():
        def step(r, carry):
            rows = pl.ds(pl.multiple_of(r * NORM_ROWS, NORM_ROWS), NORM_ROWS)
            o_ref[rows, :] = _rms(x1_ref[rows, :] + o_ref[rows, :], fg_ref[...])
            return carry

        lax.fori_loop(0, tm // NORM_ROWS, step, 0)


def _peer_ffn(hn, u_bf, v_bf, tables, x1, final_g, *, tm, te):
    t, d = hn.shape
    n_exp = u_bf.shape[0]
    tm = min(tm, t)
    rows_per_tile = te // N_KEYS
    assert SUBLANES % rows_per_tile == 0 and t % tm == 0 and n_exp % te == 0
    ni, nj = t // tm, n_exp // te
    last = ni * nj - 1

    def cur(s):
        s = jnp.minimum(s, last)
        return s // nj, s % nj

    def prev(s):
        s = jnp.maximum(s - 1, 0)
        return s // nj, s % nj

    tab_spec = pl.BlockSpec((PEER_HEADS, N_KEYS, tm), lambda s: (0, 0, prev(s)[0]))
    row_spec = pl.BlockSpec((PEER_HEADS, SUBLANES, tm),
                            lambda s: (0, (prev(s)[1] * rows_per_tile) // SUBLANES, prev(s)[0]))
    return pl.pallas_call(
        functools.partial(_peer_ffn_kernel, rows_per_tile=rows_per_tile, n_exp_tiles=nj),
        grid=(ni * nj + 1,),
        in_specs=[
            pl.BlockSpec((tm, d), lambda s: (cur(s)[0], 0)),
            pl.BlockSpec((te, d), lambda s: (cur(s)[1], 0)),
            pl.BlockSpec((te, d), lambda s: (prev(s)[1], 0)),
            row_spec, row_spec, tab_spec, tab_spec,
            _row_tile_spec((tm, d), lambda s: (prev(s)[0], 0), True),
            pl.BlockSpec((1, d), lambda s: (0, 0)),
        ],
        out_specs=pl.BlockSpec((tm, d), lambda s: (prev(s)[0], 0)),
        out_shape=jax.ShapeDtypeStruct((t, d), F32),
        scratch_shapes=[pltpu.VMEM((te, tm), F32), pltpu.VMEM((te, tm), BF16)],
        compiler_params=_cparams(("arbitrary",)),
        name="peer_ffn",
    )(hn, u_bf, v_bf, *tables, x1, final_g)


def _group(x, hist, cache_kv, p, *, n_streams):
    c = p["w_dw"].shape[1]
    a = p["g_attn"].shape[1]
    n_heads = a // HEAD_DIM
    tl = TILES
    z = _norm_matmul(x, p["mix_g"], p["w_in"], p["b_in"], tm=tl["in_proj_rows"], tn=tl["proj_cols"],
                     single_buffer=True)
    u, conv_n = _conv_module(z, hist, p["w_dw"], p["b_dw"], p["ln_g"], p["ln_b"], p["g_conv"],
                             n_streams=n_streams, tt=tl["conv_rows"])
    if cache_kv is None:
        blk = 2 * c // HEAD_DIM
        attn = _attn_prompt(z, p["frow"], n_heads=n_heads, q_blk=blk, k_blk=blk + n_heads, v_blk=blk + 2 * n_heads,
                            tq=tl["attn_rows"])
    else:
        blk = 2 * c // a
        attn = _attn_sample(z, cache_kv[0], cache_kv[1], p["frow"], n_streams=n_streams, n_heads=n_heads,
                            q_blk=blk, k_blk=blk + 1, v_blk=blk + 2)
    x1 = _merge(conv_n, attn, p["g_attn"], p["w_out"], x, tm=tl["proj_rows"], tn=tl["proj_cols"])
    q, hn = _norm_matmul(x1, p["ffn_g"], p["w_q"], p["zero_bq"], tm=tl["proj_rows"], tn=tl["proj_cols"],
                         emit_norm=True)
    tables = _peer_select(q, p["sub_keys"], tq=tl["select_tokens"])
    y = _peer_ffn(hn, p["peer_u"], p["peer_v"], tables, x1, p["final_g"], tm=tl["ffn_tokens"], te=tl["ffn_experts"])
    return y, z, u


def kernel(x_prompt, x_sample, cache_conv, cache_k, cache_v, mix_norm_g, w_in, b_in, w_dw, b_dw, ln_g, ln_b,
           rel_bias, out_norm_conv_g, out_norm_attn_g, w_out, ffn_norm_g, peer_w_q, peer_sub_keys, peer_u, peer_v,
           final_norm_g):
    assert mix_norm_g.shape[0] == 1, "single-layer step"
    bsz, seq, d = x_prompt.shape
    db, ds, _ = x_sample.shape
    c = w_dw.shape[2]
    a = out_norm_attn_g.shape[1]
    n_heads = a // HEAD_DIM
    assert bsz == 1 and seq % ATTN_Q_TILE == 0 and ds >= CONV_WIDTH - 1

    row = lambda v: v.reshape(1, -1)
    p = dict(
        mix_g=row(mix_norm_g[0]), w_in=w_in[0], b_in=row(b_in[0]),
        w_dw=w_dw[0], b_dw=row(b_dw[0]), ln_g=row(ln_g[0]), ln_b=row(ln_b[0]),
        g_conv=row(out_norm_conv_g[0]), g_attn=row(out_norm_attn_g[0]), w_out=w_out[0].astype(BF16),
        ffn_g=row(ffn_norm_g[0]), w_q=peer_w_q[0].astype(BF16),
        zero_bq=jnp.zeros((1, peer_w_q.shape[2]), F32),
        sub_keys=peer_sub_keys[0].astype(BF16), peer_u=peer_u[0].astype(BF16), peer_v=peer_v[0].astype(BF16),
        final_g=row(final_norm_g),
        frow=rel_bias[0][:, _bias_row_index()].reshape(n_heads, 1, TOEPLITZ_N),
    )

    hist_p = jnp.zeros((1, CONV_HALO, c), F32)
    y_p, z_p, u_p = _group(x_prompt.reshape(seq, d), hist_p, None, p, n_streams=1)

    pad = CONV_HALO - (CONV_WIDTH - 1)
    hist_s = jnp.pad(cache_conv[0], ((0, 0), (pad, 0), (0, 0)))
    r_len = cache_k.shape[2]
    cache_kv = (cache_k[0].reshape(db, r_len, a), cache_v[0].reshape(db, r_len, a))
    y_s, z_s, u_s = _group(x_sample.reshape(db * ds, d), hist_s, cache_kv, p, n_streams=db)

    k0, v0 = 2 * c + a, 2 * c + 2 * a
    rows_p = min(BAND_ROWS, seq)
    keep = CONV_WIDTH - 1
    return (
        y_p.reshape(1, seq, d),
        y_s.reshape(db, ds, d),
        u_p[seq - keep:].reshape(1, 1, keep, c),
        z_p[seq - rows_p:, k0:k0 + a].reshape(1, 1, rows_p, n_heads, HEAD_DIM),
        z_p[seq - rows_p:, v0:v0 + a].reshape(1, 1, rows_p, n_heads, HEAD_DIM),
        u_s.reshape(db, ds, c)[:, ds - keep:].reshape(1, db, keep, c),
        z_s[:, k0:k0 + a].reshape(1, db, ds, n_heads, HEAD_DIM),
        z_s[:, v0:v0 + a].reshape(1, db, ds, n_heads, HEAD_DIM),
    )
```

```python
import functools

import numpy as np
import jax
import jax.numpy as jnp
from jax import lax
from jax.experimental import pallas as pl
from jax.experimental.pallas import tpu as pltpu

F32 = jnp.float32
BF16 = jnp.bfloat16

CHUNK = 64
CONV_WIDTH = 31
CONV_HALO = 32
HEAD_DIM = 128
N_PREV_CHUNKS = 8
BAND_ROWS = N_PREV_CHUNKS * CHUNK
REL_CLIP = 256
ATTN_SCALE = HEAD_DIM ** -0.5
PEER_HEADS = 8
PEER_HALF = 128
N_KEYS = 128
PEER_TOPK = 16
SUBLANES = 8
MXU_WIDTH = 256
EPS = 1e-6
NEG_INF = -1e30

ATTN_Q_TILE = 2 * CHUNK
ATTN_WINDOW = BAND_ROWS + ATTN_Q_TILE
TOEPLITZ_N = 768

VMEM_LIMIT = 56 * 1024 * 1024

TILES = dict(
    proj_rows=1024, proj_cols=512,
    conv_rows=128,
    attn_rows=1024,
    select_tokens=128,
    ffn_tokens=512, ffn_experts=512,
)


def _cparams(sem, flags=None):
    return pltpu.CompilerParams(dimension_semantics=sem, vmem_limit_bytes=VMEM_LIMIT, flags=flags)


def _rms(x, g):
    ms = jnp.mean(x * x, axis=-1, keepdims=True)
    return x * lax.rsqrt(ms + EPS) * g


NORM_ROWS = 64


def _norm_rows_to(dst_ref, n_rows, row_fn):
    def step(r, carry):
        rows = pl.ds(pl.multiple_of(r * NORM_ROWS, NORM_ROWS), NORM_ROWS)
        dst_ref[rows, :] = row_fn(rows).astype(BF16)
        return carry

    lax.fori_loop(0, n_rows // NORM_ROWS, step, 0)


def _norm_matmul_kernel(x_ref, g_ref, w_ref, b_ref, o_ref, xn_ref):
    @pl.when(pl.program_id(1) == 0)
    def _():
        _norm_rows_to(xn_ref, x_ref.shape[0], lambda rows: _rms(x_ref[rows, :], g_ref[...]))

    o_ref[...] = jnp.dot(xn_ref[...], w_ref[...].astype(BF16), preferred_element_type=F32) + b_ref[...]


def _row_tile_spec(block_shape, index_map, single_buffer):
    if single_buffer:
        return pl.BlockSpec(block_shape, index_map, pipeline_mode=pl.Buffered(1))
    return pl.BlockSpec(block_shape, index_map)


def _norm_matmul(x, g, w, b, *, tm, tn, emit_norm=False, single_buffer=False):
    t, d = x.shape
    n = w.shape[1]
    tm = min(tm, t)
    assert t % tm == 0 and n % tn == 0 and tm % NORM_ROWS == 0
    out_shape = [jax.ShapeDtypeStruct((t, n), F32)]
    out_specs = [pl.BlockSpec((tm, tn), lambda i, j: (i, j))]
    scratch = [pltpu.VMEM((tm, d), BF16)]
    if emit_norm:
        out_shape.append(jax.ShapeDtypeStruct((t, d), BF16))
        out_specs.append(pl.BlockSpec((tm, d), lambda i, j: (i, 0)))
        scratch = []
    res = pl.pallas_call(
        _norm_matmul_kernel,
        grid=(t // tm, n // tn),
        in_specs=[
            _row_tile_spec((tm, d), lambda i, j: (i, 0), single_buffer),
            pl.BlockSpec((1, d), lambda i, j: (0, 0)),
            pl.BlockSpec((d, tn), lambda i, j: (0, j)),
            pl.BlockSpec((1, tn), lambda i, j: (0, j)),
        ],
        out_specs=out_specs,
        out_shape=out_shape,
        scratch_shapes=scratch,
        compiler_params=_cparams(("parallel", "arbitrary")),
        name="norm_matmul_emit" if emit_norm else "norm_matmul",
    )(x, g, w, b)
    return res if emit_norm else res[0]


def _conv_kernel(val_ref, gate_ref, hist_ref, w_ref, bdw_ref, lng_ref, lnb_ref, gc_ref,
                 u_ref, o_ref, ush_ref, par_ref, *, tt):
    t = pl.program_id(1)
    n_rows = CONV_HALO + tt
    n_ch = w_ref.shape[1]

    @pl.when((pl.program_id(0) == 0) & (t == 0))
    def _():
        for k in range(CONV_WIDTH):
            par_ref[k] = jnp.broadcast_to(w_ref[k:k + 1, :], (SUBLANES, n_ch))
        for k, ref in enumerate((bdw_ref, lng_ref, lnb_ref, gc_ref)):
            par_ref[CONV_WIDTH + k] = jnp.broadcast_to(ref[...], (SUBLANES, n_ch))

    @pl.when(t == 0)
    def _():
        ush_ref[0, 0:CONV_HALO, :] = hist_ref[0]

    @pl.when(t > 0)
    def _():
        ush_ref[0, 0:CONV_HALO, :] = ush_ref[0, tt:tt + CONV_HALO, :]

    u = val_ref[...] * jax.nn.sigmoid(gate_ref[...])
    u_ref[...] = u
    ush_ref[0, CONV_HALO:n_rows, :] = u
    upad = ush_ref[0]
    for s in range(1, SUBLANES):
        ush_ref[s] = pltpu.roll(upad, n_rows - s, 0)

    rows = SUBLANES
    for c in range(tt // rows):
        r0 = c * rows
        acc = par_ref[CONV_WIDTH]
        for k in range(CONV_WIDTH):
            off = CONV_HALO - (CONV_WIDTH - 1) + k
            lo = r0 + off - off % SUBLANES
            acc = acc + ush_ref[off % SUBLANES, lo:lo + rows, :] * par_ref[k]
        mu = jnp.mean(acc, axis=-1, keepdims=True)
        cen = acc - mu
        var = jnp.mean(cen * cen, axis=-1, keepdims=True)
        yn = cen * lax.rsqrt(var + EPS) * par_ref[CONV_WIDTH + 1] + par_ref[CONV_WIDTH + 2]
        s = yn * jax.nn.sigmoid(yn)
        o_ref[r0:r0 + rows, :] = _rms(s, par_ref[CONV_WIDTH + 3]).astype(BF16)


def _conv_module(z, hist, w_dw, b_dw, ln_g, ln_b, g_conv, *, n_streams, tt):
    t_all = z.shape[0]
    c = w_dw.shape[1]
    t_stream = t_all // n_streams
    tt = min(tt, t_stream)
    nt = t_stream // tt
    return pl.pallas_call(
        functools.partial(_conv_kernel, tt=tt),
        grid=(n_streams, nt),
        in_specs=[
            pl.BlockSpec((tt, c), lambda b, t: (b * nt + t, 0)),
            pl.BlockSpec((tt, c), lambda b, t: (b * nt + t, 1)),
            pl.BlockSpec((1, CONV_HALO, c), lambda b, t: (b, 0, 0)),
            pl.BlockSpec((CONV_WIDTH, c), lambda b, t: (0, 0)),
            pl.BlockSpec((1, c), lambda b, t: (0, 0)),
            pl.BlockSpec((1, c), lambda b, t: (0, 0)),
            pl.BlockSpec((1, c), lambda b, t: (0, 0)),
            pl.BlockSpec((1, c), lambda b, t: (0, 0)),
        ],
        out_specs=[
            pl.BlockSpec((tt, c), lambda b, t: (b * nt + t, 0)),
            pl.BlockSpec((tt, c), lambda b, t: (b * nt + t, 0)),
        ],
        out_shape=[jax.ShapeDtypeStruct((t_all, c), F32), jax.ShapeDtypeStruct((t_all, c), BF16)],
        scratch_shapes=[pltpu.VMEM((SUBLANES, CONV_HALO + tt, c), F32),
                        pltpu.VMEM((CONV_WIDTH + 4, SUBLANES, c), F32)],
        compiler_params=_cparams(("arbitrary", "arbitrary")),
        name="conv_module",
    )(z, z, hist, w_dw, b_dw, ln_g, ln_b, g_conv)


def _bias_row_index():
    m = np.arange(TOEPLITZ_N)
    dist = np.where(m <= TOEPLITZ_N - ATTN_Q_TILE, BAND_ROWS - m, BAND_ROWS + TOEPLITZ_N - m)
    return np.clip(dist, -REL_CLIP, REL_CLIP) + REL_CLIP


def _toeplitz(frow, rows):
    x = jnp.broadcast_to(frow, (rows, frow.shape[-1]))
    return pltpu.roll(x, 0, 1, stride=1, stride_axis=0)


def _attn_prompt_kernel(q_ref, k_ref, v_ref, f_ref, o_ref, kbuf_ref, vbuf_ref, bm_ref, *, t):
    i = pl.program_id(1)

    @pl.when(i == 0)
    def _():
        zeros = jnp.zeros((BAND_ROWS, HEAD_DIM), BF16)
        kbuf_ref[0:BAND_ROWS, :] = zeros
        vbuf_ref[0:BAND_ROWS, :] = zeros
        kbuf_ref[BAND_ROWS:BAND_ROWS + t, :] = k_ref[...].astype(BF16)
        vbuf_ref[BAND_ROWS:BAND_ROWS + t, :] = v_ref[...].astype(BF16)
        bias = _toeplitz(f_ref[0], ATTN_Q_TILE)[:, :ATTN_WINDOW]
        qc = lax.broadcasted_iota(jnp.int32, (ATTN_Q_TILE, ATTN_WINDOW), 0) // CHUNK
        kc = lax.broadcasted_iota(jnp.int32, (ATTN_Q_TILE, ATTN_WINDOW), 1) // CHUNK
        band = (kc >= qc) & (kc <= qc + N_PREV_CHUNKS)
        bm_ref[...] = jnp.where(band, bias, NEG_INF)

    col = lax.broadcasted_iota(jnp.int32, (ATTN_Q_TILE, ATTN_WINDOW), 1)
    for j in range(q_ref.shape[0] // ATTN_Q_TILE):
        rows = slice(j * ATTN_Q_TILE, (j + 1) * ATTN_Q_TILE)
        start = pl.multiple_of(i * q_ref.shape[0] + j * ATTN_Q_TILE, ATTN_Q_TILE)
        q = (q_ref[rows, :] * ATTN_SCALE).astype(BF16)
        kw = kbuf_ref[pl.ds(start, ATTN_WINDOW), :]
        vw = vbuf_ref[pl.ds(start, ATTN_WINDOW), :]
        s = lax.dot_general(q, kw, (((1,), (1,)), ((), ())), preferred_element_type=F32) + bm_ref[...]
        s = jnp.where(col >= BAND_ROWS - start, s, NEG_INF)
        m = jnp.max(s, axis=-1, keepdims=True)
        p = jnp.exp(s - m)
        l = jnp.sum(p, axis=-1, keepdims=True)
        o = jnp.dot(p.astype(BF16), vw, preferred_element_type=F32)
        o_ref[rows, :] = o / l


def _attn_prompt(z, frow, *, n_heads, q_blk, k_blk, v_blk, tq):
    t = z.shape[0]
    tq = min(tq, t)
    assert t % tq == 0 and tq % ATTN_Q_TILE == 0
    return pl.pallas_call(
        functools.partial(_attn_prompt_kernel, t=t),
        grid=(n_heads, t // tq),
        in_specs=[
            pl.BlockSpec((tq, HEAD_DIM), lambda h, i: (i, q_blk + h)),
            pl.BlockSpec((t, HEAD_DIM), lambda h, i: (0, k_blk + h)),
            pl.BlockSpec((t, HEAD_DIM), lambda h, i: (0, v_blk + h)),
            pl.BlockSpec((1, 1, TOEPLITZ_N), lambda h, i: (h, 0, 0)),
        ],
        out_specs=pl.BlockSpec((tq, HEAD_DIM), lambda h, i: (i, h)),
        out_shape=jax.ShapeDtypeStruct((t, n_heads * HEAD_DIM), F32),
        scratch_shapes=[
            pltpu.VMEM((BAND_ROWS + t, HEAD_DIM), BF16),
            pltpu.VMEM((BAND_ROWS + t, HEAD_DIM), BF16),
            pltpu.VMEM((ATTN_Q_TILE, ATTN_WINDOW), F32),
        ],
        compiler_params=_cparams(("arbitrary", "arbitrary")),
        name="attn_prompt",
    )(z, z, z, frow)


def _attn_sample_kernel(q_ref, k_ref, v_ref, ck_ref, cv_ref, f_ref, o_ref, bias_ref, *, n_heads, s_len, r_len):
    @pl.when(pl.program_id(0) == 0)
    def _():
        for h in range(n_heads):
            bias_ref[h] = _toeplitz(f_ref[h], s_len)

    for h in range(n_heads):
        hs = slice(h * HEAD_DIM, (h + 1) * HEAD_DIM)
        q = (q_ref[:, hs] * ATTN_SCALE).astype(BF16)
        kc = ck_ref[0, pl.ds(h, r_len, stride=n_heads), :].astype(BF16)
        vc = cv_ref[0, pl.ds(h, r_len, stride=n_heads), :].astype(BF16)
        kn = k_ref[:, hs].astype(BF16)
        vn = v_ref[:, hs].astype(BF16)
        nt = (((1,), (1,)), ((), ()))
        off = BAND_ROWS - r_len
        s_c = lax.dot_general(q, kc, nt, preferred_element_type=F32) + bias_ref[h, :, off:off + r_len]
        s_n = lax.dot_general(q, kn, nt, preferred_element_type=F32) + bias_ref[h, :, BAND_ROWS:BAND_ROWS + s_len]
        m = jnp.maximum(jnp.max(s_c, axis=-1, keepdims=True), jnp.max(s_n, axis=-1, keepdims=True))
        p_c = jnp.exp(s_c - m)
        p_n = jnp.exp(s_n - m)
        l = jnp.sum(p_c, axis=-1, keepdims=True) + jnp.sum(p_n, axis=-1, keepdims=True)
        o = (jnp.dot(p_c.astype(BF16), vc, preferred_element_type=F32)
             + jnp.dot(p_n.astype(BF16), vn, preferred_element_type=F32))
        o_ref[:, hs] = o / l


def _attn_sample(z, cache_k, cache_v, frow, *, n_streams, n_heads, q_blk, k_blk, v_blk):
    s_len = z.shape[0] // n_streams
    r_len = cache_k.shape[1] // n_heads
    a = n_heads * HEAD_DIM
    return pl.pallas_call(
        functools.partial(_attn_sample_kernel, n_heads=n_heads, s_len=s_len, r_len=r_len),
        grid=(n_streams,),
        in_specs=[
            pl.BlockSpec((s_len, a), lambda b: (b, q_blk)),
            pl.BlockSpec((s_len, a), lambda b: (b, k_blk)),
            pl.BlockSpec((s_len, a), lambda b: (b, v_blk)),
            pl.BlockSpec((1, r_len * n_heads, HEAD_DIM), lambda b: (b, 0, 0)),
            pl.BlockSpec((1, r_len * n_heads, HEAD_DIM), lambda b: (b, 0, 0)),
            pl.BlockSpec((n_heads, 1, TOEPLITZ_N), lambda b: (0, 0, 0)),
        ],
        out_specs=pl.BlockSpec((s_len, a), lambda b: (b, 0)),
        out_shape=jax.ShapeDtypeStruct((z.shape[0], a), F32),
        scratch_shapes=[pltpu.VMEM((n_heads, s_len, TOEPLITZ_N), F32)],
        compiler_params=_cparams(("arbitrary",)),
        name="attn_sample",
    )(z, z, z, cache_k, cache_v, frow)


def _merge_kernel(cn_ref, at_ref, ga_ref, w_ref, x_ref, o_ref, an_ref):
    c = cn_ref.shape[1]

    @pl.when(pl.program_id(1) == 0)
    def _():
        _norm_rows_to(an_ref, at_ref.shape[0], lambda rows: _rms(at_ref[rows, :], ga_ref[...]))

    o_ref[...] = (x_ref[...]
                  + jnp.dot(cn_ref[...], w_ref[0:c, :].astype(BF16), preferred_element_type=F32)
                  + jnp.dot(an_ref[...], w_ref[c:, :].astype(BF16), preferred_element_type=F32))


def _merge(conv_n, attn, g_attn, w_out, x, *, tm, tn):
    t, d = x.shape
    c = conv_n.shape[1]
    a = attn.shape[1]
    tm = min(tm, t)
    assert t % tm == 0 and d % tn == 0 and tm % NORM_ROWS == 0
    return pl.pallas_call(
        _merge_kernel,
        grid=(t // tm, d // tn),
        in_specs=[
            pl.BlockSpec((tm, c), lambda i, j: (i, 0)),
            pl.BlockSpec((tm, a), lambda i, j: (i, 0)),
            pl.BlockSpec((1, a), lambda i, j: (0, 0)),
            pl.BlockSpec((c + a, tn), lambda i, j: (0, j)),
            pl.BlockSpec((tm, tn), lambda i, j: (i, j)),
        ],
        out_specs=pl.BlockSpec((tm, tn), lambda i, j: (i, j)),
        out_shape=jax.ShapeDtypeStruct((t, d), F32),
        scratch_shapes=[pltpu.VMEM((tm, a), BF16)],
        compiler_params=_cparams(("parallel", "arbitrary")),
        name="merge",
    )(conv_n, attn, g_attn, w_out, x)


def _top16(scores, pos, n_pos):
    lanes = pos.shape[1]
    slot = lax.broadcasted_iota(jnp.int32, (PEER_TOPK, lanes), 0)

    def body(k, carry):
        out = []
        for s, vals, picks in carry:
            m = jnp.max(s, axis=0, keepdims=True)
            idx = jnp.min(jnp.where(s == m, pos, n_pos), axis=0, keepdims=True)
            s = jnp.where(pos == idx, -jnp.inf, s)
            vals = jnp.where(slot == k, m, vals)
            picks = jnp.where(slot == k, idx, picks)
            out.append((s, vals, picks))
        return tuple(out)

    init = tuple((s, jnp.zeros((PEER_TOPK, lanes), F32), jnp.zeros((PEER_TOPK, lanes), jnp.int32)) for s in scores)
    return lax.fori_loop(0, PEER_TOPK, body, init)


_GRID_HALF = PEER_TOPK // 2
SELECT_GROUP = 2


def _candidate_grid(v1, v2, iota8):
    neg = -jnp.inf
    rows = [v1[0:1, :] + v2]
    for a in range(1, _GRID_HALF):
        rows.append(jnp.where(iota8 < PEER_TOPK // (a + 1), v1[a:a + 1, :] + v2[0:_GRID_HALF, :], neg))
    rows.append(v1[_GRID_HALF:, :] + v2[0:1, :])
    return jnp.concatenate(rows, axis=0)


def _candidate_pos(iota8):
    pos = [iota8, iota8 + _GRID_HALF]
    pos += [iota8 + a * PEER_TOPK for a in range(1, _GRID_HALF)]
    pos.append((iota8 + _GRID_HALF) * PEER_TOPK)
    return jnp.concatenate(pos, axis=0)


def _kept_counts(left, iota8):
    gone = (left == -jnp.inf).astype(F32)
    cnt = [jnp.sum(gone[0:PEER_TOPK, :], axis=0, keepdims=True)]
    for a in range(1, _GRID_HALF):
        r0 = PEER_TOPK + (a - 1) * _GRID_HALF
        piece = jnp.where(iota8 < PEER_TOPK // (a + 1), gone[r0:r0 + _GRID_HALF, :], 0.0)
        cnt.append(jnp.sum(piece, axis=0, keepdims=True))
    r0 = PEER_TOPK + (_GRID_HALF - 1) * _GRID_HALF
    cnt += [gone[r0 + i:r0 + i + 1, :] for i in range(_GRID_HALF)]
    return cnt


def _peer_select_kernel(q_ref, keys_ref, cnt_ref, e1_ref, r2_ref, e2_ref, *, group):
    lanes = q_ref.shape[0]
    key_row = lax.broadcasted_iota(jnp.int32, (N_KEYS, lanes), 0)
    iota8 = lax.broadcasted_iota(jnp.int32, (_GRID_HALF, lanes), 0)
    grid_pos = _candidate_pos(iota8)
    nt = (((1,), (1,)), ((), ()))
    for h0 in range(0, PEER_HEADS, group):
        heads = range(h0, h0 + group)
        scores = []
        for h in heads:
            for p, e_ref in ((0, e1_ref), (1, e2_ref)):
                c0 = (h * 2 + p) * PEER_HALF
                qhp = q_ref[:, c0:c0 + PEER_HALF].astype(BF16)
                s = lax.dot_general(keys_ref[h, p], qhp, nt, preferred_element_type=F32)
                e_ref[h] = jnp.exp(s - jnp.max(s, axis=0, keepdims=True)).astype(e_ref.dtype)
                scores.append(s)
        found = _top16(scores, key_row, N_KEYS)
        grids = []
        for i, h in enumerate(heads):
            (_, v1, _), (_, v2, picks2) = found[2 * i], found[2 * i + 1]
            r2 = jnp.full((N_KEYS, lanes), float(PEER_TOPK), F32)
            for k in range(PEER_TOPK):
                r2 = jnp.where(key_row == picks2[k:k + 1, :], float(k), r2)
            r2_ref[h] = r2.astype(r2_ref.dtype)
            grids.append(_candidate_grid(v1, v2, iota8))
        best = _top16(grids, grid_pos, PEER_TOPK * PEER_TOPK)
        for i, h in enumerate(heads):
            left, vals, _ = best[i]
            z = jnp.sum(jnp.exp(vals - vals[0:1, :]), axis=0, keepdims=True)
            cnt_a = _kept_counts(left, iota8)
            picks1 = found[2 * i][2]
            cnt = jnp.zeros((N_KEYS, lanes), F32)
            for k in range(PEER_TOPK):
                cnt = jnp.where(key_row == picks1[k:k + 1, :], cnt_a[k], cnt)
            cnt_ref[h] = cnt
            e1_ref[h] = e1_ref[h] * (1.0 / z)


def _peer_select(q, sub_keys, *, tq):
    t = q.shape[0]
    tq = min(tq, t)
    tab = jax.ShapeDtypeStruct((PEER_HEADS, N_KEYS, t), F32)
    tab_bf = jax.ShapeDtypeStruct((PEER_HEADS, N_KEYS, t), BF16)
    tab_spec = pl.BlockSpec((PEER_HEADS, N_KEYS, tq), lambda i: (0, 0, i))
    return pl.pallas_call(
        functools.partial(_peer_select_kernel, group=SELECT_GROUP),
        grid=(t // tq,),
        in_specs=[
            pl.BlockSpec((tq, q.shape[1]), lambda i: (i, 0)),
            pl.BlockSpec(sub_keys.shape, lambda i: (0, 0, 0, 0)),
        ],
        out_specs=[tab_spec] * 4,
        out_shape=[tab, tab, tab_bf, tab_bf],
        compiler_params=_cparams(("parallel",)),
        name="peer_select",
    )(q, sub_keys)


def _peer_ffn_kernel(hn_ref, u_ref, v_ref, cnt_ref, e1_ref, r2_ref, e2_ref, x1_ref, fg_ref, o_ref, act_ref, a_ref,
                     *, rows_per_tile, n_exp_tiles):
    s = pl.program_id(0)
    jp = jnp.maximum(s - 1, 0) % n_exp_tiles

    @pl.when(s == 0)
    def _():
        act_ref[...] = jnp.zeros_like(act_ref)

    @pl.when(jp == 0)
    def _():
        o_ref[...] = jnp.zeros_like(o_ref)

    tm = act_ref.shape[1]

    def finish_rows(r):
        row = (jp * rows_per_tile) % SUBLANES + r
        gate = jnp.zeros((N_KEYS, tm), BF16)
        for h in range(PEER_HEADS):
            cnt = cnt_ref[h, pl.ds(row, 1), :].astype(BF16)
            e1 = e1_ref[h, pl.ds(row, 1), :].astype(BF16)
            gate = gate + jnp.where(r2_ref[h] < cnt, e2_ref[h], jnp.zeros((), BF16)) * e1
        x = act_ref[r * N_KEYS:(r + 1) * N_KEYS, :]
        gelu = 0.5 * x * (1.0 + lax.erf(x * (2.0 ** -0.5)))
        a_ref[r * N_KEYS:(r + 1) * N_KEYS, :] = (gate.astype(F32) * gelu).astype(BF16)

    nt = (((1,), (1,)), ((), ()))
    act_new = lax.dot_general(u_ref[...], hn_ref[...], nt, preferred_element_type=F32)
    for r in range(rows_per_tile):
        finish_rows(r)
    tn = (((0,), (0,)), ((), ()))
    o_ref[...] += lax.dot_general(a_ref[...], v_ref[...], tn, preferred_element_type=F32)
    act_ref[...] = act_new

    @pl.when(jp == n_exp_tiles - 1)
    def _():
        def step(r, carry):
            rows = pl.ds(pl.multiple_of(r * NORM_ROWS, NORM_ROWS), NORM_ROWS)
            o_ref[rows, :] = _rms(x1_ref[rows, :] + o_ref[rows, :], fg_ref[...])
            return carry

        lax.fori_loop(0, tm // NORM_ROWS, step, 0)


def _peer_ffn(hn, u_bf, v_bf, tables, x1, final_g, *, tm, te):
    t, d = hn.shape
    n_exp = u_bf.shape[0]
    tm = min(tm, t)
    rows_per_tile = te // N_KEYS
    assert SUBLANES % rows_per_tile == 0 and t % tm == 0 and n_exp % te == 0
    ni, nj = t // tm, n_exp // te
    last = ni * nj - 1

    def cur(s):
        s = jnp.minimum(s, last)
        return s // nj, s % nj

    def prev(s):
        s = jnp.maximum(s - 1, 0)
        return s // nj, s % nj

    tab_spec = pl.BlockSpec((PEER_HEADS, N_KEYS, tm), lambda s: (0, 0, prev(s)[0]))
    row_spec = pl.BlockSpec((PEER_HEADS, SUBLANES, tm),
                            lambda s: (0, (prev(s)[1] * rows_per_tile) // SUBLANES, prev(s)[0]))
    return pl.pallas_call(
        functools.partial(_peer_ffn_kernel, rows_per_tile=rows_per_tile, n_exp_tiles=nj),
        grid=(ni * nj + 1,),
        in_specs=[
            pl.BlockSpec((tm, d), lambda s: (cur(s)[0], 0)),
            pl.BlockSpec((te, d), lambda s: (cur(s)[1], 0)),
            pl.BlockSpec((te, d), lambda s: (prev(s)[1], 0)),
            row_spec, row_spec, tab_spec, tab_spec,
            _row_tile_spec((tm, d), lambda s: (prev(s)[0], 0), True),
            pl.BlockSpec((1, d), lambda s: (0, 0)),
        ],
        out_specs=pl.BlockSpec((tm, d), lambda s: (prev(s)[0], 0)),
        out_shape=jax.ShapeDtypeStruct((t, d), F32),
        scratch_shapes=[pltpu.VMEM((te, tm), F32), pltpu.VMEM((te, tm), BF16)],
        compiler_params=_cparams(("arbitrary",)),
        name="peer_ffn",
    )(hn, u_bf, v_bf, *tables, x1, final_g)


def _group(x, hist, cache_kv, p, *, n_streams):
    c = p["w_dw"].shape[1]
    a = p["g_attn"].shape[1]
    n_heads = a // HEAD_DIM
    tl = TILES
    z = _norm_matmul(x, p["mix_g"], p["w_in"], p["b_in"], tm=tl["proj_rows"], tn=tl["proj_cols"],
                     single_buffer=True)
    u, conv_n = _conv_module(z, hist, p["w_dw"], p["b_dw"], p["ln_g"], p["ln_b"], p["g_conv"],
                             n_streams=n_streams, tt=tl["conv_rows"])
    if cache_kv is None:
        blk = 2 * c // HEAD_DIM
        attn = _attn_prompt(z, p["frow"], n_heads=n_heads, q_blk=blk, k_blk=blk + n_heads, v_blk=blk + 2 * n_heads,
                            tq=tl["attn_rows"])
    else:
        blk = 2 * c // a
        attn = _attn_sample(z, cache_kv[0], cache_kv[1], p["frow"], n_streams=n_streams, n_heads=n_heads,
                            q_blk=blk, k_blk=blk + 1, v_blk=blk + 2)
    x1 = _merge(conv_n, attn, p["g_attn"], p["w_out"], x, tm=tl["proj_rows"], tn=tl["proj_cols"])
    q, hn = _norm_matmul(x1, p["ffn_g"], p["w_q"], p["zero_bq"], tm=tl["proj_rows"], tn=tl["proj_cols"],
                         emit_norm=True, single_buffer=True)
    tables = _peer_select(q, p["sub_keys"], tq=tl["select_tokens"])
    y = _peer_ffn(hn, p["peer_u"], p["peer_v"], tables, x1, p["final_g"], tm=tl["ffn_tokens"], te=tl["ffn_experts"])
    return y, z, u


def kernel(x_prompt, x_sample, cache_conv, cache_k, cache_v, mix_norm_g, w_in, b_in, w_dw, b_dw, ln_g, ln_b,
           rel_bias, out_norm_conv_g, out_norm_attn_g, w_out, ffn_norm_g, peer_w_q, peer_sub_keys, peer_u, peer_v,
           final_norm_g):
    assert mix_norm_g.shape[0] == 1, "single-layer step"
    bsz, seq, d = x_prompt.shape
    db, ds, _ = x_sample.shape
    c = w_dw.shape[2]
    a = out_norm_attn_g.shape[1]
    n_heads = a // HEAD_DIM
    assert bsz == 1 and seq % ATTN_Q_TILE == 0 and ds >= CONV_WIDTH - 1

    row = lambda v: v.reshape(1, -1)
    p = dict(
        mix_g=row(mix_norm_g[0]), w_in=w_in[0], b_in=row(b_in[0]),
        w_dw=w_dw[0], b_dw=row(b_dw[0]), ln_g=row(ln_g[0]), ln_b=row(ln_b[0]),
        g_conv=row(out_norm_conv_g[0]), g_attn=row(out_norm_attn_g[0]), w_out=w_out[0].astype(BF16),
        ffn_g=row(ffn_norm_g[0]), w_q=peer_w_q[0].astype(BF16),
        zero_bq=jnp.zeros((1, peer_w_q.shape[2]), F32),
        sub_keys=peer_sub_keys[0].astype(BF16), peer_u=peer_u[0].astype(BF16), peer_v=peer_v[0].astype(BF16),
        final_g=row(final_norm_g),
        frow=rel_bias[0][:, _bias_row_index()].reshape(n_heads, 1, TOEPLITZ_N),
    )

    hist_p = jnp.zeros((1, CONV_HALO, c), F32)
    y_p, z_p, u_p = _group(x_prompt.reshape(seq, d), hist_p, None, p, n_streams=1)

    pad = CONV_HALO - (CONV_WIDTH - 1)
    hist_s = jnp.pad(cache_conv[0], ((0, 0), (pad, 0), (0, 0)))
    r_len = cache_k.shape[2]
    cache_kv = (cache_k[0].reshape(db, r_len * n_heads, HEAD_DIM), cache_v[0].reshape(db, r_len * n_heads, HEAD_DIM))
    y_s, z_s, u_s = _group(x_sample.reshape(db * ds, d), hist_s, cache_kv, p, n_streams=db)

    k0, v0 = 2 * c + a, 2 * c + 2 * a
    rows_p = min(BAND_ROWS, seq)
    keep = CONV_WIDTH - 1
    return (
        y_p.reshape(1, seq, d),
        y_s.reshape(db, ds, d),
        u_p[seq - keep:].reshape(1, 1, keep, c),
        z_p[seq - rows_p:, k0:k0 + a].reshape(1, 1, rows_p, n_heads, HEAD_DIM),
        z_p[seq - rows_p:, v0:v0 + a].reshape(1, 1, rows_p, n_heads, HEAD_DIM),
        u_s.reshape(db, ds, c)[:, ds - keep:].reshape(1, db, keep, c),
        z_s[:, k0:k0 + a].reshape(1, db, ds, n_heads, HEAD_DIM),
        z_s[:, v0:v0 + a].reshape(1, db, ds, n_heads, HEAD_DIM),
    )
```

```python
import functools

import numpy as np
import jax
import jax.numpy as jnp
from jax import lax
from jax.experimental import pallas as pl
from jax.experimental.pallas import tpu as pltpu

F32 = jnp.float32
BF16 = jnp.bfloat16

CHUNK = 64
CONV_WIDTH = 31
CONV_HALO = 32
HEAD_DIM = 128
N_PREV_CHUNKS = 8
BAND_ROWS = N_PREV_CHUNKS * CHUNK
REL_CLIP = 256
ATTN_SCALE = HEAD_DIM ** -0.5
PEER_HEADS = 8
PEER_HALF = 128
N_KEYS = 128
PEER_TOPK = 16
SUBLANES = 8
MXU_WIDTH = 256
EPS = 1e-6
NEG_INF = -1e30

ATTN_Q_TILE = 4 * CHUNK
ATTN_WINDOW = BAND_ROWS + ATTN_Q_TILE
TOEPLITZ_N = 1024

VMEM_LIMIT = 56 * 1024 * 1024

TILES = dict(
    proj_rows=1024, proj_cols=512,
    conv_rows=128,
    attn_rows=1024,
    select_tokens=128,
    ffn_tokens=512, ffn_experts=512,
)


def _cparams(sem, flags=None):
    return pltpu.CompilerParams(dimension_semantics=sem, vmem_limit_bytes=VMEM_LIMIT, flags=flags)


def _rms(x, g):
    ms = jnp.mean(x * x, axis=-1, keepdims=True)
    return x * lax.rsqrt(ms + EPS) * g


def _slab_specs(mats, n_steps, index_map):
    specs, shapes = [], []
    for m in mats:
        rows = m.shape[0] // n_steps
        assert m.shape[0] % n_steps == 0 and rows % 16 == 0, (m.shape, n_steps)
        specs.append(pl.BlockSpec((rows, m.shape[1]), index_map))
        shapes.append(jax.ShapeDtypeStruct(m.shape, BF16))
    return specs, shapes


def _cast_slabs(src_refs, dst_refs):
    for src, dst in zip(src_refs, dst_refs, strict=True):
        dst[...] = src[...].astype(BF16)


NORM_ROWS = 64


def _norm_rows_to(dst_ref, n_rows, row_fn):
    def step(r, carry):
        rows = pl.ds(pl.multiple_of(r * NORM_ROWS, NORM_ROWS), NORM_ROWS)
        dst_ref[rows, :] = row_fn(rows).astype(BF16)
        return carry

    lax.fori_loop(0, n_rows // NORM_ROWS, step, 0)


def _norm_matmul_kernel(x_ref, g_ref, w_ref, b_ref, o_ref, xn_ref):
    @pl.when(pl.program_id(1) == 0)
    def _():
        _norm_rows_to(xn_ref, x_ref.shape[0], lambda rows: _rms(x_ref[rows, :], g_ref[...]))

    o_ref[...] = jnp.dot(xn_ref[...], w_ref[...].astype(BF16), preferred_element_type=F32) + b_ref[...]


def _row_tile_spec(block_shape, index_map, single_buffer):
    if single_buffer:
        return pl.BlockSpec(block_shape, index_map, pipeline_mode=pl.Buffered(1))
    return pl.BlockSpec(block_shape, index_map)


def _norm_matmul(x, g, w, b, *, tm, tn, emit_norm=False, single_buffer=False):
    t, d = x.shape
    n = w.shape[1]
    tm = min(tm, t)
    assert t % tm == 0 and n % tn == 0 and tm % NORM_ROWS == 0
    out_shape = [jax.ShapeDtypeStruct((t, n), F32)]
    out_specs = [pl.BlockSpec((tm, tn), lambda i, j: (i, j))]
    scratch = [pltpu.VMEM((tm, d), BF16)]
    if emit_norm:
        out_shape.append(jax.ShapeDtypeStruct((t, d), BF16))
        out_specs.append(pl.BlockSpec((tm, d), lambda i, j: (i, 0)))
        scratch = []
    res = pl.pallas_call(
        _norm_matmul_kernel,
        grid=(t // tm, n // tn),
        in_specs=[
            _row_tile_spec((tm, d), lambda i, j: (i, 0), single_buffer),
            pl.BlockSpec((1, d), lambda i, j: (0, 0)),
            pl.BlockSpec((d, tn), lambda i, j: (0, j)),
            pl.BlockSpec((1, tn), lambda i, j: (0, j)),
        ],
        out_specs=out_specs,
        out_shape=out_shape,
        scratch_shapes=scratch,
        compiler_params=_cparams(("parallel", "arbitrary")),
        name="norm_matmul_emit" if emit_norm else "norm_matmul",
    )(x, g, w, b)
    return res if emit_norm else res[0]


def _conv_kernel(val_ref, gate_ref, hist_ref, w_ref, bdw_ref, lng_ref, lnb_ref, gc_ref, *refs, tt, n_cast):
    u_ref, o_ref = refs[n_cast:n_cast + 2]
    ush_ref, par_ref = refs[2 * n_cast + 2:]
    _cast_slabs(refs[:n_cast], refs[n_cast + 2:2 * n_cast + 2])
    t = pl.program_id(1)
    n_rows = CONV_HALO + tt
    n_ch = w_ref.shape[1]

    @pl.when((pl.program_id(0) == 0) & (t == 0))
    def _():
        for k in range(CONV_WIDTH):
            par_ref[k] = jnp.broadcast_to(w_ref[k:k + 1, :], (SUBLANES, n_ch))
        for k, ref in enumerate((bdw_ref, lng_ref, lnb_ref, gc_ref)):
            par_ref[CONV_WIDTH + k] = jnp.broadcast_to(ref[...], (SUBLANES, n_ch))

    @pl.when(t == 0)
    def _():
        ush_ref[0, 0:CONV_HALO, :] = hist_ref[0]

    @pl.when(t > 0)
    def _():
        ush_ref[0, 0:CONV_HALO, :] = ush_ref[0, tt:tt + CONV_HALO, :]

    u = val_ref[...] * jax.nn.sigmoid(gate_ref[...])
    u_ref[...] = u
    ush_ref[0, CONV_HALO:n_rows, :] = u
    upad = ush_ref[0]
    for s in range(1, SUBLANES):
        ush_ref[s] = pltpu.roll(upad, n_rows - s, 0)

    rows = SUBLANES
    for c in range(tt // rows):
        r0 = c * rows
        acc = par_ref[CONV_WIDTH]
        for k in range(CONV_WIDTH):
            off = CONV_HALO - (CONV_WIDTH - 1) + k
            lo = r0 + off - off % SUBLANES
            acc = acc + ush_ref[off % SUBLANES, lo:lo + rows, :] * par_ref[k]
        mu = jnp.mean(acc, axis=-1, keepdims=True)
        cen = acc - mu
        var = jnp.mean(cen * cen, axis=-1, keepdims=True)
        yn = cen * lax.rsqrt(var + EPS) * par_ref[CONV_WIDTH + 1] + par_ref[CONV_WIDTH + 2]
        s = yn * jax.nn.sigmoid(yn)
        o_ref[r0:r0 + rows, :] = _rms(s, par_ref[CONV_WIDTH + 3]).astype(BF16)


def _conv_module(z, hist, w_dw, b_dw, ln_g, ln_b, g_conv, *, n_streams, tt, to_bf16=()):
    t_all = z.shape[0]
    c = w_dw.shape[1]
    t_stream = t_all // n_streams
    tt = min(tt, t_stream)
    nt = t_stream // tt
    slab_specs, slab_shapes = _slab_specs(to_bf16, n_streams * nt, lambda b, t: (b * nt + t, 0))
    res = pl.pallas_call(
        functools.partial(_conv_kernel, tt=tt, n_cast=len(to_bf16)),
        grid=(n_streams, nt),
        in_specs=[
            pl.BlockSpec((tt, c), lambda b, t: (b * nt + t, 0)),
            pl.BlockSpec((tt, c), lambda b, t: (b * nt + t, 1)),
            pl.BlockSpec((1, CONV_HALO, c), lambda b, t: (b, 0, 0)),
            pl.BlockSpec((CONV_WIDTH, c), lambda b, t: (0, 0)),
            pl.BlockSpec((1, c), lambda b, t: (0, 0)),
            pl.BlockSpec((1, c), lambda b, t: (0, 0)),
            pl.BlockSpec((1, c), lambda b, t: (0, 0)),
            pl.BlockSpec((1, c), lambda b, t: (0, 0)),
        ] + slab_specs,
        out_specs=[
            pl.BlockSpec((tt, c), lambda b, t: (b * nt + t, 0)),
            pl.BlockSpec((tt, c), lambda b, t: (b * nt + t, 0)),
        ] + slab_specs,
        out_shape=[jax.ShapeDtypeStruct((t_all, c), F32), jax.ShapeDtypeStruct((t_all, c), BF16)] + slab_shapes,
        scratch_shapes=[pltpu.VMEM((SUBLANES, CONV_HALO + tt, c), F32),
                        pltpu.VMEM((CONV_WIDTH + 4, SUBLANES, c), F32)],
        compiler_params=_cparams(("arbitrary", "arbitrary")),
        name="conv_module_cast" if to_bf16 else "conv_module",
    )(z, z, hist, w_dw, b_dw, ln_g, ln_b, g_conv, *to_bf16)
    return res[0], res[1], res[2:]


def _bias_row_index():
    m = np.arange(TOEPLITZ_N)
    dist = np.where(m <= TOEPLITZ_N - ATTN_Q_TILE, BAND_ROWS - m, BAND_ROWS + TOEPLITZ_N - m)
    return np.clip(dist, -REL_CLIP, REL_CLIP) + REL_CLIP


def _toeplitz(frow, rows):
    x = jnp.broadcast_to(frow, (rows, frow.shape[-1]))
    return pltpu.roll(x, 0, 1, stride=1, stride_axis=0)


def _attn_prompt_kernel(q_ref, k_ref, v_ref, f_ref, o_ref, kbuf_ref, vbuf_ref, bm_ref, *, t):
    i = pl.program_id(1)

    @pl.when(i == 0)
    def _():
        zeros = jnp.zeros((BAND_ROWS, HEAD_DIM), BF16)
        kbuf_ref[0:BAND_ROWS, :] = zeros
        vbuf_ref[0:BAND_ROWS, :] = zeros
        kbuf_ref[BAND_ROWS:BAND_ROWS + t, :] = k_ref[...].astype(BF16)
        vbuf_ref[BAND_ROWS:BAND_ROWS + t, :] = v_ref[...].astype(BF16)
        bias = _toeplitz(f_ref[0], ATTN_Q_TILE)[:, :ATTN_WINDOW]
        qc = lax.broadcasted_iota(jnp.int32, (ATTN_Q_TILE, ATTN_WINDOW), 0) // CHUNK
        kc = lax.broadcasted_iota(jnp.int32, (ATTN_Q_TILE, ATTN_WINDOW), 1) // CHUNK
        band = (kc >= qc) & (kc <= qc + N_PREV_CHUNKS)
        bm_ref[...] = jnp.where(band, bias, NEG_INF)

    col = lax.broadcasted_iota(jnp.int32, (ATTN_Q_TILE, ATTN_WINDOW), 1)
    for j in range(q_ref.shape[0] // ATTN_Q_TILE):
        rows = slice(j * ATTN_Q_TILE, (j + 1) * ATTN_Q_TILE)
        start = pl.multiple_of(i * q_ref.shape[0] + j * ATTN_Q_TILE, ATTN_Q_TILE)
        q = (q_ref[rows, :] * ATTN_SCALE).astype(BF16)
        kw = kbuf_ref[pl.ds(start, ATTN_WINDOW), :]
        vw = vbuf_ref[pl.ds(start, ATTN_WINDOW), :]
        s = lax.dot_general(q, kw, (((1,), (1,)), ((), ())), preferred_element_type=F32) + bm_ref[...]
        s = jnp.where(col >= BAND_ROWS - start, s, NEG_INF)
        m = jnp.max(s, axis=-1, keepdims=True)
        p = jnp.exp(s - m)
        l = jnp.sum(p, axis=-1, keepdims=True)
        o = jnp.dot(p.astype(BF16), vw, preferred_element_type=F32)
        o_ref[rows, :] = o / l


def _attn_prompt(z, frow, *, n_heads, q_blk, k_blk, v_blk, tq):
    t = z.shape[0]
    tq = min(tq, t)
    assert t % tq == 0 and tq % ATTN_Q_TILE == 0
    return pl.pallas_call(
        functools.partial(_attn_prompt_kernel, t=t),
        grid=(n_heads, t // tq),
        in_specs=[
            pl.BlockSpec((tq, HEAD_DIM), lambda h, i: (i, q_blk + h)),
            pl.BlockSpec((t, HEAD_DIM), lambda h, i: (0, k_blk + h)),
            pl.BlockSpec((t, HEAD_DIM), lambda h, i: (0, v_blk + h)),
            pl.BlockSpec((1, 1, TOEPLITZ_N), lambda h, i: (h, 0, 0)),
        ],
        out_specs=pl.BlockSpec((tq, HEAD_DIM), lambda h, i: (i, h)),
        out_shape=jax.ShapeDtypeStruct((t, n_heads * HEAD_DIM), F32),
        scratch_shapes=[
            pltpu.VMEM((BAND_ROWS + t, HEAD_DIM), BF16),
            pltpu.VMEM((BAND_ROWS + t, HEAD_DIM), BF16),
            pltpu.VMEM((ATTN_Q_TILE, ATTN_WINDOW), F32),
        ],
        compiler_params=_cparams(("arbitrary", "arbitrary")),
        name="attn_prompt",
    )(z, z, z, frow)


def _attn_sample_kernel(q_ref, k_ref, v_ref, ck_ref, cv_ref, f_ref, o_ref, bias_ref, *, n_heads, s_len, r_len):
    @pl.when(pl.program_id(0) == 0)
    def _():
        for h in range(n_heads):
            bias_ref[h] = _toeplitz(f_ref[h], s_len)

    for h in range(n_heads):
        hs = slice(h * HEAD_DIM, (h + 1) * HEAD_DIM)
        q = (q_ref[:, hs] * ATTN_SCALE).astype(BF16)
        kc = ck_ref[0, pl.ds(h, r_len, stride=n_heads), :].astype(BF16)
        vc = cv_ref[0, pl.ds(h, r_len, stride=n_heads), :].astype(BF16)
        kn = k_ref[:, hs].astype(BF16)
        vn = v_ref[:, hs].astype(BF16)
        nt = (((1,), (1,)), ((), ()))
        off = BAND_ROWS - r_len
        s_c = lax.dot_general(q, kc, nt, preferred_element_type=F32) + bias_ref[h, :, off:off + r_len]
        s_n = lax.dot_general(q, kn, nt, preferred_element_type=F32) + bias_ref[h, :, BAND_ROWS:BAND_ROWS + s_len]
        m = jnp.maximum(jnp.max(s_c, axis=-1, keepdims=True), jnp.max(s_n, axis=-1, keepdims=True))
        p_c = jnp.exp(s_c - m)
        p_n = jnp.exp(s_n - m)
        l = jnp.sum(p_c, axis=-1, keepdims=True) + jnp.sum(p_n, axis=-1, keepdims=True)
        o = (jnp.dot(p_c.astype(BF16), vc, preferred_element_type=F32)
             + jnp.dot(p_n.astype(BF16), vn, preferred_element_type=F32))
        o_ref[:, hs] = o / l


def _attn_sample(z, cache_k, cache_v, frow, *, n_streams, n_heads, q_blk, k_blk, v_blk):
    s_len = z.shape[0] // n_streams
    r_len = cache_k.shape[1] // n_heads
    a = n_heads * HEAD_DIM
    return pl.pallas_call(
        functools.partial(_attn_sample_kernel, n_heads=n_heads, s_len=s_len, r_len=r_len),
        grid=(n_streams,),
        in_specs=[
            pl.BlockSpec((s_len, a), lambda b: (b, q_blk)),
            pl.BlockSpec((s_len, a), lambda b: (b, k_blk)),
            pl.BlockSpec((s_len, a), lambda b: (b, v_blk)),
            pl.BlockSpec((1, r_len * n_heads, HEAD_DIM), lambda b: (b, 0, 0)),
            pl.BlockSpec((1, r_len * n_heads, HEAD_DIM), lambda b: (b, 0, 0)),
            pl.BlockSpec((n_heads, 1, TOEPLITZ_N), lambda b: (0, 0, 0)),
        ],
        out_specs=pl.BlockSpec((s_len, a), lambda b: (b, 0)),
        out_shape=jax.ShapeDtypeStruct((z.shape[0], a), F32),
        scratch_shapes=[pltpu.VMEM((n_heads, s_len, TOEPLITZ_N), F32)],
        compiler_params=_cparams(("arbitrary",)),
        name="attn_sample",
    )(z, z, z, cache_k, cache_v, frow)


def _merge_kernel(cn_ref, at_ref, ga_ref, w_ref, x_ref, o_ref, an_ref):
    c = cn_ref.shape[1]

    @pl.when(pl.program_id(1) == 0)
    def _():
        _norm_rows_to(an_ref, at_ref.shape[0], lambda rows: _rms(at_ref[rows, :], ga_ref[...]))

    o_ref[...] = (x_ref[...]
                  + jnp.dot(cn_ref[...], w_ref[0:c, :].astype(BF16), preferred_element_type=F32)
                  + jnp.dot(an_ref[...], w_ref[c:, :].astype(BF16), preferred_element_type=F32))


def _merge(conv_n, attn, g_attn, w_out, x, *, tm, tn):
    t, d = x.shape
    c = conv_n.shape[1]
    a = attn.shape[1]
    tm = min(tm, t)
    assert t % tm == 0 and d % tn == 0 and tm % NORM_ROWS == 0
    return pl.pallas_call(
        _merge_kernel,
        grid=(t // tm, d // tn),
        in_specs=[
            pl.BlockSpec((tm, c), lambda i, j: (i, 0)),
            pl.BlockSpec((tm, a), lambda i, j: (i, 0)),
            pl.BlockSpec((1, a), lambda i, j: (0, 0)),
            pl.BlockSpec((c + a, tn), lambda i, j: (0, j)),
            pl.BlockSpec((tm, tn), lambda i, j: (i, j)),
        ],
        out_specs=pl.BlockSpec((tm, tn), lambda i, j: (i, j)),
        out_shape=jax.ShapeDtypeStruct((t, d), F32),
        scratch_shapes=[pltpu.VMEM((tm, a), BF16)],
        compiler_params=_cparams(("parallel", "arbitrary")),
        name="merge",
    )(conv_n, attn, g_attn, w_out, x)


def _top16(scores, pos, n_pos):
    lanes = pos.shape[1]
    slot = lax.broadcasted_iota(jnp.int32, (PEER_TOPK, lanes), 0)

    def body(k, carry):
        out = []
        for s, vals, picks in carry:
            m = jnp.max(s, axis=0, keepdims=True)
            idx = jnp.min(jnp.where(s == m, pos, n_pos), axis=0, keepdims=True)
            s = jnp.where(pos == idx, -jnp.inf, s)
            vals = jnp.where(slot == k, m, vals)
            picks = jnp.where(slot == k, idx, picks)
            out.append((s, vals, picks))
        return tuple(out)

    init = tuple((s, jnp.zeros((PEER_TOPK, lanes), F32), jnp.zeros((PEER_TOPK, lanes), jnp.int32)) for s in scores)
    return lax.fori_loop(0, PEER_TOPK, body, init)


_GRID_HALF = PEER_TOPK // 2
SELECT_GROUP = 2


def _candidate_grid(v1, v2, iota8):
    neg = -jnp.inf
    rows = [v1[0:1, :] + v2]
    for a in range(1, _GRID_HALF):
        rows.append(jnp.where(iota8 < PEER_TOPK // (a + 1), v1[a:a + 1, :] + v2[0:_GRID_HALF, :], neg))
    rows.append(v1[_GRID_HALF:, :] + v2[0:1, :])
    return jnp.concatenate(rows, axis=0)


def _candidate_pos(iota8):
    pos = [iota8, iota8 + _GRID_HALF]
    pos += [iota8 + a * PEER_TOPK for a in range(1, _GRID_HALF)]
    pos.append((iota8 + _GRID_HALF) * PEER_TOPK)
    return jnp.concatenate(pos, axis=0)


def _kept_counts(left, iota8):
    gone = (left == -jnp.inf).astype(F32)
    cnt = [jnp.sum(gone[0:PEER_TOPK, :], axis=0, keepdims=True)]
    for a in range(1, _GRID_HALF):
        r0 = PEER_TOPK + (a - 1) * _GRID_HALF
        piece = jnp.where(iota8 < PEER_TOPK // (a + 1), gone[r0:r0 + _GRID_HALF, :], 0.0)
        cnt.append(jnp.sum(piece, axis=0, keepdims=True))
    r0 = PEER_TOPK + (_GRID_HALF - 1) * _GRID_HALF
    cnt += [gone[r0 + i:r0 + i + 1, :] for i in range(_GRID_HALF)]
    return cnt


def _peer_select_kernel(q_ref, keys_ref, *refs, group, n_cast):
    cnt_ref, e1_ref, r2_ref, e2_ref = refs[n_cast:n_cast + 4]
    _cast_slabs(refs[:n_cast], refs[n_cast + 4:])
    lanes = q_ref.shape[0]
    key_row = lax.broadcasted_iota(jnp.int32, (N_KEYS, lanes), 0)
    iota8 = lax.broadcasted_iota(jnp.int32, (_GRID_HALF, lanes), 0)
    grid_pos = _candidate_pos(iota8)
    nt = (((1,), (1,)), ((), ()))
    for h0 in range(0, PEER_HEADS, group):
        heads = range(h0, h0 + group)
        scores = []
        for h in heads:
            for p, e_ref in ((0, e1_ref), (1, e2_ref)):
                c0 = (h * 2 + p) * PEER_HALF
                qhp = q_ref[:, c0:c0 + PEER_HALF].astype(BF16)
                s = lax.dot_general(keys_ref[h, p], qhp, nt, preferred_element_type=F32)
                e_ref[h] = jnp.exp(s - jnp.max(s, axis=0, keepdims=True)).astype(e_ref.dtype)
                scores.append(s)
        found = _top16(scores, key_row, N_KEYS)
        grids = []
        for i, h in enumerate(heads):
            (_, v1, _), (_, v2, picks2) = found[2 * i], found[2 * i + 1]
            r2 = jnp.full((N_KEYS, lanes), float(PEER_TOPK), F32)
            for k in range(PEER_TOPK):
                r2 = jnp.where(key_row == picks2[k:k + 1, :], float(k), r2)
            r2_ref[h] = r2.astype(r2_ref.dtype)
            grids.append(_candidate_grid(v1, v2, iota8))
        best = _top16(grids, grid_pos, PEER_TOPK * PEER_TOPK)
        for i, h in enumerate(heads):
            left, vals, _ = best[i]
            z = jnp.sum(jnp.exp(vals - vals[0:1, :]), axis=0, keepdims=True)
            cnt_a = _kept_counts(left, iota8)
            picks1 = found[2 * i][2]
            cnt = jnp.zeros((N_KEYS, lanes), F32)
            for k in range(PEER_TOPK):
                cnt = jnp.where(key_row == picks1[k:k + 1, :], cnt_a[k], cnt)
            cnt_ref[h] = cnt
            e1_ref[h] = e1_ref[h] * (1.0 / z)


def _peer_select(q, sub_keys, *, tq, to_bf16=()):
    t = q.shape[0]
    tq = min(tq, t)
    n_steps = t // tq
    tab = jax.ShapeDtypeStruct((PEER_HEADS, N_KEYS, t), F32)
    tab_bf = jax.ShapeDtypeStruct((PEER_HEADS, N_KEYS, t), BF16)
    tab_spec = pl.BlockSpec((PEER_HEADS, N_KEYS, tq), lambda i: (0, 0, i))
    slab_specs, slab_shapes = _slab_specs(to_bf16, n_steps, lambda i: (i, 0))
    res = pl.pallas_call(
        functools.partial(_peer_select_kernel, group=SELECT_GROUP, n_cast=len(to_bf16)),
        grid=(n_steps,),
        in_specs=[
            pl.BlockSpec((tq, q.shape[1]), lambda i: (i, 0)),
            pl.BlockSpec(sub_keys.shape, lambda i: (0, 0, 0, 0)),
        ] + slab_specs,
        out_specs=[tab_spec] * 4 + slab_specs,
        out_shape=[tab, tab, tab_bf, tab_bf] + slab_shapes,
        compiler_params=_cparams(("arbitrary",)),
        name="peer_select_cast" if to_bf16 else "peer_select",
    )(q, sub_keys, *to_bf16)
    return res[:4], res[4:]


def _peer_ffn_kernel(hn_ref, u_ref, v_ref, cnt_ref, e1_ref, r2_ref, e2_ref, x1_ref, fg_ref, o_ref, act_ref, a_ref,
                     *, rows_per_tile, n_exp_tiles):
    s = pl.program_id(0)
    jp = jnp.maximum(s - 1, 0) % n_exp_tiles

    @pl.when(s == 0)
    def _():
        act_ref[...] = jnp.zeros_like(act_ref)

    @pl.when(jp == 0)
    def _():
        o_ref[...] = jnp.zeros_like(o_ref)

    tm = act_ref.shape[1]

    def finish_rows(r):
        row = (jp * rows_per_tile) % SUBLANES + r
        gate = jnp.zeros((N_KEYS, tm), BF16)
        for h in range(PEER_HEADS):
            cnt = cnt_ref[h, pl.ds(row, 1), :].astype(BF16)
            e1 = e1_ref[h, pl.ds(row, 1), :].astype(BF16)
            gate = gate + jnp.where(r2_ref[h] < cnt, e2_ref[h], jnp.zeros((), BF16)) * e1
        x = act_ref[r * N_KEYS:(r + 1) * N_KEYS, :]
        gelu = 0.5 * x * (1.0 + lax.erf(x * (2.0 ** -0.5)))
        a_ref[r * N_KEYS:(r + 1) * N_KEYS, :] = (gate.astype(F32) * gelu).astype(BF16)

    nt = (((1,), (1,)), ((), ()))
    act_new = lax.dot_general(u_ref[...], hn_ref[...], nt, preferred_element_type=F32)
    for r in range(rows_per_tile):
        finish_rows(r)
    tn = (((0,), (0,)), ((), ()))
    o_ref[...] += lax.dot_general(a_ref[...], v_ref[...], tn, preferred_element_type=F32)
    act_ref[...] = act_new

    @pl.when(jp == n_exp_tiles - 1)
    def _():
        def step(r, carry):
            rows = pl.ds(pl.multiple_of(r * NORM_ROWS, NORM_ROWS), NORM_ROWS)
            o_ref[rows, :] = _rms(x1_ref[rows, :] + o_ref[rows, :], fg_ref[...])
            return carry

        lax.fori_loop(0, tm // NORM_ROWS, step, 0)


def _peer_ffn(hn, u_bf, v_bf, tables, x1, final_g, *, tm, te):
    t, d = hn.shape
    n_exp = u_bf.shape[0]
    tm = min(tm, t)
    rows_per_tile = te // N_KEYS
    assert SUBLANES % rows_per_tile == 0 and t % tm == 0 and n_exp % te == 0
    ni, nj = t // tm, n_exp // te
    last = ni * nj - 1

    def cur(s):
        s = jnp.minimum(s, last)
        return s // nj, s % nj

    def prev(s):
        s = jnp.maximum(s - 1, 0)
        return s // nj, s % nj

    tab_spec = pl.BlockSpec((PEER_HEADS, N_KEYS, tm), lambda s: (0, 0, prev(s)[0]))
    row_spec = pl.BlockSpec((PEER_HEADS, SUBLANES, tm),
                            lambda s: (0, (prev(s)[1] * rows_per_tile) // SUBLANES, prev(s)[0]))
    return pl.pallas_call(
        functools.partial(_peer_ffn_kernel, rows_per_tile=rows_per_tile, n_exp_tiles=nj),
        grid=(ni * nj + 1,),
        in_specs=[
            pl.BlockSpec((tm, d), lambda s: (cur(s)[0], 0)),
            pl.BlockSpec((te, d), lambda s: (cur(s)[1], 0)),
            pl.BlockSpec((te, d), lambda s: (prev(s)[1], 0)),
            row_spec, row_spec, tab_spec, tab_spec,
            _row_tile_spec((tm, d), lambda s: (prev(s)[0], 0), True),
            pl.BlockSpec((1, d), lambda s: (0, 0)),
        ],
        out_specs=pl.BlockSpec((tm, d), lambda s: (prev(s)[0], 0)),
        out_shape=jax.ShapeDtypeStruct((t, d), F32),
        scratch_shapes=[pltpu.VMEM((te, tm), F32), pltpu.VMEM((te, tm), BF16)],
        compiler_params=_cparams(("arbitrary",)),
        name="peer_ffn",
    )(hn, u_bf, v_bf, *tables, x1, final_g)


def _group(x, hist, cache_kv, p, *, n_streams):
    c = p["w_dw"].shape[1]
    a = p["g_attn"].shape[1]
    n_heads = a // HEAD_DIM
    tl = TILES
    z = _norm_matmul(x, p["mix_g"], p["w_in"], p["b_in"], tm=tl["proj_rows"], tn=tl["proj_cols"],
                     single_buffer=True)
    pending = (p["w_out"], p["w_q"]) if p["w_out"].dtype != BF16 else ()
    u, conv_n, done = _conv_module(z, hist, p["w_dw"], p["b_dw"], p["ln_g"], p["ln_b"], p["g_conv"],
                                   n_streams=n_streams, tt=tl["conv_rows"], to_bf16=pending)
    if done:
        p = dict(p, w_out=done[0], w_q=done[1])
    if cache_kv is None:
        blk = 2 * c // HEAD_DIM
        attn = _attn_prompt(z, p["frow"], n_heads=n_heads, q_blk=blk, k_blk=blk + n_heads, v_blk=blk + 2 * n_heads,
                            tq=tl["attn_rows"])
    else:
        blk = 2 * c // a
        attn = _attn_sample(z, cache_kv[0], cache_kv[1], p["frow"], n_streams=n_streams, n_heads=n_heads,
                            q_blk=blk, k_blk=blk + 1, v_blk=blk + 2)
    x1 = _merge(conv_n, attn, p["g_attn"], p["w_out"], x, tm=tl["proj_rows"], tn=tl["proj_cols"])
    q, hn = _norm_matmul(x1, p["ffn_g"], p["w_q"], p["zero_bq"], tm=tl["proj_rows"], tn=tl["proj_cols"],
                         emit_norm=True, single_buffer=True)
    pending = (p["peer_u"], p["peer_v"]) if p["peer_u"].dtype != BF16 else ()
    tables, done = _peer_select(q, p["sub_keys"], tq=tl["select_tokens"], to_bf16=pending)
    if done:
        p = dict(p, peer_u=done[0], peer_v=done[1])
    y = _peer_ffn(hn, p["peer_u"], p["peer_v"], tables, x1, p["final_g"], tm=tl["ffn_tokens"], te=tl["ffn_experts"])
    return y, z, u, p


def kernel(x_prompt, x_sample, cache_conv, cache_k, cache_v, mix_norm_g, w_in, b_in, w_dw, b_dw, ln_g, ln_b,
           rel_bias, out_norm_conv_g, out_norm_attn_g, w_out, ffn_norm_g, peer_w_q, peer_sub_keys, peer_u, peer_v,
           final_norm_g):
    assert mix_norm_g.shape[0] == 1, "single-layer step"
    bsz, seq, d = x_prompt.shape
    db, ds, _ = x_sample.shape
    c = w_dw.shape[2]
    a = out_norm_attn_g.shape[1]
    n_heads = a // HEAD_DIM
    assert bsz == 1 and seq % ATTN_Q_TILE == 0 and ds >= CONV_WIDTH - 1

    row = lambda v: v.reshape(1, -1)
    p = dict(
        mix_g=row(mix_norm_g[0]), w_in=w_in[0], b_in=row(b_in[0]),
        w_dw=w_dw[0], b_dw=row(b_dw[0]), ln_g=row(ln_g[0]), ln_b=row(ln_b[0]),
        g_conv=row(out_norm_conv_g[0]), g_attn=row(out_norm_attn_g[0]), w_out=w_out[0],
        ffn_g=row(ffn_norm_g[0]), w_q=peer_w_q[0],
        zero_bq=jnp.zeros((1, peer_w_q.shape[2]), F32),
        sub_keys=peer_sub_keys[0].astype(BF16), peer_u=peer_u[0], peer_v=peer_v[0],
        final_g=row(final_norm_g),
        frow=rel_bias[0][:, _bias_row_index()].reshape(n_heads, 1, TOEPLITZ_N),
    )

    hist_p = jnp.zeros((1, CONV_HALO, c), F32)
    y_p, z_p, u_p, p = _group(x_prompt.reshape(seq, d), hist_p, None, p, n_streams=1)

    pad = CONV_HALO - (CONV_WIDTH - 1)
    hist_s = jnp.pad(cache_conv[0], ((0, 0), (pad, 0), (0, 0)))
    r_len = cache_k.shape[2]
    cache_kv = (cache_k[0].reshape(db, r_len * n_heads, HEAD_DIM), cache_v[0].reshape(db, r_len * n_heads, HEAD_DIM))
    y_s, z_s, u_s, _ = _group(x_sample.reshape(db * ds, d), hist_s, cache_kv, p, n_streams=db)

    k0, v0 = 2 * c + a, 2 * c + 2 * a
    rows_p = min(BAND_ROWS, seq)
    keep = CONV_WIDTH - 1
    return (
        y_p.reshape(1, seq, d),
        y_s.reshape(db, ds, d),
        u_p[seq - keep:].reshape(1, 1, keep, c),
        z_p[seq - rows_p:, k0:k0 + a].reshape(1, 1, rows_p, n_heads, HEAD_DIM),
        z_p[seq - rows_p:, v0:v0 + a].reshape(1, 1, rows_p, n_heads, HEAD_DIM),
        u_s.reshape(db, ds, c)[:, ds - keep:].reshape(1, db, keep, c),
        z_s[:, k0:k0 + a].reshape(1, db, ds, n_heads, HEAD_DIM),
        z_s[:, v0:v0 + a].reshape(1, db, ds, n_heads, HEAD_DIM),
    )
```

```python
import functools

import numpy as np
import jax
import jax.numpy as jnp
from jax import lax
from jax.experimental import pallas as pl
from jax.experimental.pallas import tpu as pltpu

F32 = jnp.float32
BF16 = jnp.bfloat16

CHUNK = 64
CONV_WIDTH = 31
CONV_HALO = 32
HEAD_DIM = 128
N_PREV_CHUNKS = 8
BAND_ROWS = N_PREV_CHUNKS * CHUNK
REL_CLIP = 256
ATTN_SCALE = HEAD_DIM ** -0.5
PEER_HEADS = 8
PEER_HALF = 128
N_KEYS = 128
PEER_TOPK = 16
SUBLANES = 8
LANES = 128
MXU_WIDTH = 256
EPS = 1e-6
NEG_INF = -1e30

ATTN_Q_TILE = 4 * CHUNK
ATTN_WINDOW = BAND_ROWS + ATTN_Q_TILE
TOEPLITZ_N = 1024

VMEM_LIMIT = 56 * 1024 * 1024

TILES = dict(
    proj_rows=1024, proj_cols=512,
    conv_rows=128,
    attn_rows=1024,
    select_tokens=128,
    ffn_tokens=512, ffn_experts=512,
)


def _cparams(sem, flags=None):
    return pltpu.CompilerParams(dimension_semantics=sem, vmem_limit_bytes=VMEM_LIMIT, flags=flags)


def _rms(x, g):
    ms = jnp.mean(x * x, axis=-1, keepdims=True)
    return x * lax.rsqrt(ms + EPS) * g


def _slab_specs(mats, n_steps, index_map):
    specs, shapes = [], []
    for m in mats:
        rows = m.shape[0] // n_steps
        assert m.shape[0] % n_steps == 0 and rows % 16 == 0, (m.shape, n_steps)
        specs.append(pl.BlockSpec((rows, m.shape[1]), index_map))
        shapes.append(jax.ShapeDtypeStruct(m.shape, BF16))
    return specs, shapes


def _cast_slabs(src_refs, dst_refs):
    for src, dst in zip(src_refs, dst_refs, strict=True):
        dst[...] = src[...].astype(BF16)


NORM_ROWS = 64


def _norm_rows_to(dst_ref, n_rows, row_fn):
    def step(r, carry):
        rows = pl.ds(pl.multiple_of(r * NORM_ROWS, NORM_ROWS), NORM_ROWS)
        dst_ref[rows, :] = row_fn(rows).astype(BF16)
        return carry

    lax.fori_loop(0, n_rows // NORM_ROWS, step, 0)


def _norm_matmul_kernel(x_ref, g_ref, w_ref, b_ref, o_ref, *rest):
    xn_ref = rest[-1]

    @pl.when(pl.program_id(1) == 0)
    def _():
        _norm_rows_to(xn_ref, x_ref.shape[0], lambda rows: _rms(x_ref[rows, :], g_ref[...]))
        if len(rest) == 2:
            def step(r, carry):
                rows = pl.ds(pl.multiple_of(r * LANES, LANES), LANES)
                rest[0][:, rows] = xn_ref[rows, :].astype(F32).T.astype(BF16)
                return carry

            lax.fori_loop(0, x_ref.shape[0] // LANES, step, 0)

    o_ref[...] = jnp.dot(xn_ref[...], w_ref[...].astype(BF16), preferred_element_type=F32) + b_ref[...]


def _row_tile_spec(block_shape, index_map, single_buffer):
    if single_buffer:
        return pl.BlockSpec(block_shape, index_map, pipeline_mode=pl.Buffered(1))
    return pl.BlockSpec(block_shape, index_map)


def _norm_matmul(x, g, w, b, *, tm, tn, emit_norm=False, single_buffer=False):
    t, d = x.shape
    n = w.shape[1]
    tm = min(tm, t)
    assert t % tm == 0 and n % tn == 0 and tm % NORM_ROWS == 0
    out_shape = [jax.ShapeDtypeStruct((t, n), F32)]
    out_specs = [pl.BlockSpec((tm, tn), lambda i, j: (i, j))]
    scratch = [pltpu.VMEM((tm, d), BF16)]
    if emit_norm:
        assert tm % LANES == 0
        out_shape.append(jax.ShapeDtypeStruct((d, t), BF16))
        out_specs.append(pl.BlockSpec((d, tm), lambda i, j: (0, i)))
    res = pl.pallas_call(
        _norm_matmul_kernel,
        grid=(t // tm, n // tn),
        in_specs=[
            _row_tile_spec((tm, d), lambda i, j: (i, 0), single_buffer),
            pl.BlockSpec((1, d), lambda i, j: (0, 0)),
            pl.BlockSpec((d, tn), lambda i, j: (0, j)),
            pl.BlockSpec((1, tn), lambda i, j: (0, j)),
        ],
        out_specs=out_specs,
        out_shape=out_shape,
        scratch_shapes=scratch,
        compiler_params=_cparams(("parallel", "arbitrary")),
        name="norm_matmul_emit" if emit_norm else "norm_matmul",
    )(x, g, w, b)
    return res if emit_norm else res[0]


def _conv_kernel(val_ref, gate_ref, hist_ref, w_ref, bdw_ref, lng_ref, lnb_ref, gc_ref, *refs, tt, n_cast):
    u_ref, o_ref = refs[n_cast:n_cast + 2]
    ush_ref, par_ref = refs[2 * n_cast + 2:]
    _cast_slabs(refs[:n_cast], refs[n_cast + 2:2 * n_cast + 2])
    t = pl.program_id(1)
    n_rows = CONV_HALO + tt
    n_ch = w_ref.shape[1]

    @pl.when((pl.program_id(0) == 0) & (t == 0))
    def _():
        for k in range(CONV_WIDTH):
            par_ref[k] = jnp.broadcast_to(w_ref[k:k + 1, :], (SUBLANES, n_ch))
        for k, ref in enumerate((bdw_ref, lng_ref, lnb_ref, gc_ref)):
            par_ref[CONV_WIDTH + k] = jnp.broadcast_to(ref[...], (SUBLANES, n_ch))

    @pl.when(t == 0)
    def _():
        ush_ref[0, 0:CONV_HALO, :] = hist_ref[0]

    @pl.when(t > 0)
    def _():
        ush_ref[0, 0:CONV_HALO, :] = ush_ref[0, tt:tt + CONV_HALO, :]

    u = val_ref[...] * jax.nn.sigmoid(gate_ref[...])
    u_ref[...] = u
    ush_ref[0, CONV_HALO:n_rows, :] = u
    upad = ush_ref[0]
    for s in range(1, SUBLANES):
        ush_ref[s] = pltpu.roll(upad, n_rows - s, 0)

    rows = SUBLANES
    for c in range(tt // rows):
        r0 = c * rows
        acc = par_ref[CONV_WIDTH]
        for k in range(CONV_WIDTH):
            off = CONV_HALO - (CONV_WIDTH - 1) + k
            lo = r0 + off - off % SUBLANES
            acc = acc + ush_ref[off % SUBLANES, lo:lo + rows, :] * par_ref[k]
        mu = jnp.mean(acc, axis=-1, keepdims=True)
        cen = acc - mu
        var = jnp.mean(cen * cen, axis=-1, keepdims=True)
        yn = cen * lax.rsqrt(var + EPS) * par_ref[CONV_WIDTH + 1] + par_ref[CONV_WIDTH + 2]
        s = yn * jax.nn.sigmoid(yn)
        o_ref[r0:r0 + rows, :] = _rms(s, par_ref[CONV_WIDTH + 3]).astype(BF16)


def _conv_module(z, hist, w_dw, b_dw, ln_g, ln_b, g_conv, *, n_streams, tt, to_bf16=()):
    t_all = z.shape[0]
    c = w_dw.shape[1]
    t_stream = t_all // n_streams
    tt = min(tt, t_stream)
    nt = t_stream // tt
    slab_specs, slab_shapes = _slab_specs(to_bf16, n_streams * nt, lambda b, t: (b * nt + t, 0))
    res = pl.pallas_call(
        functools.partial(_conv_kernel, tt=tt, n_cast=len(to_bf16)),
        grid=(n_streams, nt),
        in_specs=[
            pl.BlockSpec((tt, c), lambda b, t: (b * nt + t, 0)),
            pl.BlockSpec((tt, c), lambda b, t: (b * nt + t, 1)),
            pl.BlockSpec((1, CONV_HALO, c), lambda b, t: (b, 0, 0)),
            pl.BlockSpec((CONV_WIDTH, c), lambda b, t: (0, 0)),
            pl.BlockSpec((1, c), lambda b, t: (0, 0)),
            pl.BlockSpec((1, c), lambda b, t: (0, 0)),
            pl.BlockSpec((1, c), lambda b, t: (0, 0)),
            pl.BlockSpec((1, c), lambda b, t: (0, 0)),
        ] + slab_specs,
        out_specs=[
            pl.BlockSpec((tt, c), lambda b, t: (b * nt + t, 0)),
            pl.BlockSpec((tt, c), lambda b, t: (b * nt + t, 0)),
        ] + slab_specs,
        out_shape=[jax.ShapeDtypeStruct((t_all, c), F32), jax.ShapeDtypeStruct((t_all, c), BF16)] + slab_shapes,
        scratch_shapes=[pltpu.VMEM((SUBLANES, CONV_HALO + tt, c), F32),
                        pltpu.VMEM((CONV_WIDTH + 4, SUBLANES, c), F32)],
        compiler_params=_cparams(("arbitrary", "arbitrary")),
        name="conv_module_cast" if to_bf16 else "conv_module",
    )(z, z, hist, w_dw, b_dw, ln_g, ln_b, g_conv, *to_bf16)
    return res[0], res[1], res[2:]


def _bias_row_index():
    m = np.arange(TOEPLITZ_N)
    dist = np.where(m <= TOEPLITZ_N - ATTN_Q_TILE, BAND_ROWS - m, BAND_ROWS + TOEPLITZ_N - m)
    return np.clip(dist, -REL_CLIP, REL_CLIP) + REL_CLIP


def _toeplitz(frow, rows):
    x = jnp.broadcast_to(frow, (rows, frow.shape[-1]))
    return pltpu.roll(x, 0, 1, stride=1, stride_axis=0)


def _attn_prompt_kernel(q_ref, k_ref, v_ref, f_ref, o_ref, kbuf_ref, vbuf_ref, bm_ref, *, t):
    i = pl.program_id(1)

    @pl.when(i == 0)
    def _():
        zeros = jnp.zeros((BAND_ROWS, HEAD_DIM), BF16)
        kbuf_ref[0:BAND_ROWS, :] = zeros
        vbuf_ref[0:BAND_ROWS, :] = zeros
        kbuf_ref[BAND_ROWS:BAND_ROWS + t, :] = k_ref[...].astype(BF16)
        vbuf_ref[BAND_ROWS:BAND_ROWS + t, :] = v_ref[...].astype(BF16)
        bias = _toeplitz(f_ref[0], ATTN_Q_TILE)[:, :ATTN_WINDOW]
        qc = lax.broadcasted_iota(jnp.int32, (ATTN_Q_TILE, ATTN_WINDOW), 0) // CHUNK
        kc = lax.broadcasted_iota(jnp.int32, (ATTN_Q_TILE, ATTN_WINDOW), 1) // CHUNK
        band = (kc >= qc) & (kc <= qc + N_PREV_CHUNKS)
        bm_ref[...] = jnp.where(band, bias, NEG_INF)

    col = lax.broadcasted_iota(jnp.int32, (ATTN_Q_TILE, ATTN_WINDOW), 1)
    for j in range(q_ref.shape[0] // ATTN_Q_TILE):
        rows = slice(j * ATTN_Q_TILE, (j + 1) * ATTN_Q_TILE)
        start = pl.multiple_of(i * q_ref.shape[0] + j * ATTN_Q_TILE, ATTN_Q_TILE)
        q = (q_ref[rows, :] * ATTN_SCALE).astype(BF16)
        kw = kbuf_ref[pl.ds(start, ATTN_WINDOW), :]
        vw = vbuf_ref[pl.ds(start, ATTN_WINDOW), :]
        s = lax.dot_general(q, kw, (((1,), (1,)), ((), ())), preferred_element_type=F32) + bm_ref[...]
        s = jnp.where(col >= BAND_ROWS - start, s, NEG_INF)
        m = jnp.max(s, axis=-1, keepdims=True)
        p = jnp.exp(s - m)
        l = jnp.sum(p, axis=-1, keepdims=True)
        o = jnp.dot(p.astype(BF16), vw, preferred_element_type=F32)
        o_ref[rows, :] = o / l


def _attn_prompt(z, frow, *, n_heads, q_blk, k_blk, v_blk, tq):
    t = z.shape[0]
    tq = min(tq, t)
    assert t % tq == 0 and tq % ATTN_Q_TILE == 0
    return pl.pallas_call(
        functools.partial(_attn_prompt_kernel, t=t),
        grid=(n_heads, t // tq),
        in_specs=[
            pl.BlockSpec((tq, HEAD_DIM), lambda h, i: (i, q_blk + h)),
            pl.BlockSpec((t, HEAD_DIM), lambda h, i: (0, k_blk + h)),
            pl.BlockSpec((t, HEAD_DIM), lambda h, i: (0, v_blk + h)),
            pl.BlockSpec((1, 1, TOEPLITZ_N), lambda h, i: (h, 0, 0)),
        ],
        out_specs=pl.BlockSpec((tq, HEAD_DIM), lambda h, i: (i, h)),
        out_shape=jax.ShapeDtypeStruct((t, n_heads * HEAD_DIM), F32),
        scratch_shapes=[
            pltpu.VMEM((BAND_ROWS + t, HEAD_DIM), BF16),
            pltpu.VMEM((BAND_ROWS + t, HEAD_DIM), BF16),
            pltpu.VMEM((ATTN_Q_TILE, ATTN_WINDOW), F32),
        ],
        compiler_params=_cparams(("arbitrary", "arbitrary")),
        name="attn_prompt",
    )(z, z, z, frow)


def _attn_sample_kernel(q_ref, k_ref, v_ref, ck_ref, cv_ref, f_ref, o_ref, bias_ref, *, n_heads, s_len, r_len):
    @pl.when(pl.program_id(0) == 0)
    def _():
        for h in range(n_heads):
            bias_ref[h] = _toeplitz(f_ref[h], s_len)

    for h in range(n_heads):
        hs = slice(h * HEAD_DIM, (h + 1) * HEAD_DIM)
        q = (q_ref[:, hs] * ATTN_SCALE).astype(BF16)
        kc = ck_ref[0, pl.ds(h, r_len, stride=n_heads), :].astype(BF16)
        vc = cv_ref[0, pl.ds(h, r_len, stride=n_heads), :].astype(BF16)
        kn = k_ref[:, hs].astype(BF16)
        vn = v_ref[:, hs].astype(BF16)
        nt = (((1,), (1,)), ((), ()))
        off = BAND_ROWS - r_len
        s_c = lax.dot_general(q, kc, nt, preferred_element_type=F32) + bias_ref[h, :, off:off + r_len]
        s_n = lax.dot_general(q, kn, nt, preferred_element_type=F32) + bias_ref[h, :, BAND_ROWS:BAND_ROWS + s_len]
        m = jnp.maximum(jnp.max(s_c, axis=-1, keepdims=True), jnp.max(s_n, axis=-1, keepdims=True))
        p_c = jnp.exp(s_c - m)
        p_n = jnp.exp(s_n - m)
        l = jnp.sum(p_c, axis=-1, keepdims=True) + jnp.sum(p_n, axis=-1, keepdims=True)
        o = (jnp.dot(p_c.astype(BF16), vc, preferred_element_type=F32)
             + jnp.dot(p_n.astype(BF16), vn, preferred_element_type=F32))
        o_ref[:, hs] = o / l


def _attn_sample(z, cache_k, cache_v, frow, *, n_streams, n_heads, q_blk, k_blk, v_blk):
    s_len = z.shape[0] // n_streams
    r_len = cache_k.shape[1] // n_heads
    a = n_heads * HEAD_DIM
    return pl.pallas_call(
        functools.partial(_attn_sample_kernel, n_heads=n_heads, s_len=s_len, r_len=r_len),
        grid=(n_streams,),
        in_specs=[
            pl.BlockSpec((s_len, a), lambda b: (b, q_blk)),
            pl.BlockSpec((s_len, a), lambda b: (b, k_blk)),
            pl.BlockSpec((s_len, a), lambda b: (b, v_blk)),
            pl.BlockSpec((1, r_len * n_heads, HEAD_DIM), lambda b: (b, 0, 0)),
            pl.BlockSpec((1, r_len * n_heads, HEAD_DIM), lambda b: (b, 0, 0)),
            pl.BlockSpec((n_heads, 1, TOEPLITZ_N), lambda b: (0, 0, 0)),
        ],
        out_specs=pl.BlockSpec((s_len, a), lambda b: (b, 0)),
        out_shape=jax.ShapeDtypeStruct((z.shape[0], a), F32),
        scratch_shapes=[pltpu.VMEM((n_heads, s_len, TOEPLITZ_N), F32)],
        compiler_params=_cparams(("arbitrary",)),
        name="attn_sample",
    )(z, z, z, cache_k, cache_v, frow)


def _merge_kernel(cn_ref, at_ref, ga_ref, w_ref, x_ref, o_ref, an_ref):
    c = cn_ref.shape[1]

    @pl.when(pl.program_id(1) == 0)
    def _():
        _norm_rows_to(an_ref, at_ref.shape[0], lambda rows: _rms(at_ref[rows, :], ga_ref[...]))

    o_ref[...] = (x_ref[...]
                  + jnp.dot(cn_ref[...], w_ref[0:c, :].astype(BF16), preferred_element_type=F32)
                  + jnp.dot(an_ref[...], w_ref[c:, :].astype(BF16), preferred_element_type=F32))


def _merge(conv_n, attn, g_attn, w_out, x, *, tm, tn):
    t, d = x.shape
    c = conv_n.shape[1]
    a = attn.shape[1]
    tm = min(tm, t)
    assert t % tm == 0 and d % tn == 0 and tm % NORM_ROWS == 0
    return pl.pallas_call(
        _merge_kernel,
        grid=(t // tm, d // tn),
        in_specs=[
            pl.BlockSpec((tm, c), lambda i, j: (i, 0)),
            pl.BlockSpec((tm, a), lambda i, j: (i, 0)),
            pl.BlockSpec((1, a), lambda i, j: (0, 0)),
            pl.BlockSpec((c + a, tn), lambda i, j: (0, j)),
            pl.BlockSpec((tm, tn), lambda i, j: (i, j)),
        ],
        out_specs=pl.BlockSpec((tm, tn), lambda i, j: (i, j)),
        out_shape=jax.ShapeDtypeStruct((t, d), F32),
        scratch_shapes=[pltpu.VMEM((tm, a), BF16)],
        compiler_params=_cparams(("parallel", "arbitrary")),
        name="merge",
    )(conv_n, attn, g_attn, w_out, x)


def _top16(scores, pos, n_pos):
    lanes = pos.shape[1]
    slot = lax.broadcasted_iota(jnp.int32, (PEER_TOPK, lanes), 0)

    def body(k, carry):
        out = []
        for s, vals, picks in carry:
            m = jnp.max(s, axis=0, keepdims=True)
            idx = jnp.min(jnp.where(s == m, pos, n_pos), axis=0, keepdims=True)
            s = jnp.where(pos == idx, -jnp.inf, s)
            vals = jnp.where(slot == k, m, vals)
            picks = jnp.where(slot == k, idx, picks)
            out.append((s, vals, picks))
        return tuple(out)

    init = tuple((s, jnp.zeros((PEER_TOPK, lanes), F32), jnp.zeros((PEER_TOPK, lanes), jnp.int32)) for s in scores)
    return lax.fori_loop(0, PEER_TOPK, body, init)


_GRID_HALF = PEER_TOPK // 2
SELECT_GROUP = 2


def _candidate_grid(v1, v2, iota8):
    neg = -jnp.inf
    rows = [v1[0:1, :] + v2]
    for a in range(1, _GRID_HALF):
        rows.append(jnp.where(iota8 < PEER_TOPK // (a + 1), v1[a:a + 1, :] + v2[0:_GRID_HALF, :], neg))
    rows.append(v1[_GRID_HALF:, :] + v2[0:1, :])
    return jnp.concatenate(rows, axis=0)


def _candidate_pos(iota8):
    pos = [iota8, iota8 + _GRID_HALF]
    pos += [iota8 + a * PEER_TOPK for a in range(1, _GRID_HALF)]
    pos.append((iota8 + _GRID_HALF) * PEER_TOPK)
    return jnp.concatenate(pos, axis=0)


def _kept_counts(left, iota8):
    gone = (left == -jnp.inf).astype(F32)
    cnt = [jnp.sum(gone[0:PEER_TOPK, :], axis=0, keepdims=True)]
    for a in range(1, _GRID_HALF):
        r0 = PEER_TOPK + (a - 1) * _GRID_HALF
        piece = jnp.where(iota8 < PEER_TOPK // (a + 1), gone[r0:r0 + _GRID_HALF, :], 0.0)
        cnt.append(jnp.sum(piece, axis=0, keepdims=True))
    r0 = PEER_TOPK + (_GRID_HALF - 1) * _GRID_HALF
    cnt += [gone[r0 + i:r0 + i + 1, :] for i in range(_GRID_HALF)]
    return cnt


def _peer_select_kernel(q_ref, keys_ref, *refs, group, n_cast):
    cnt_ref, e1_ref, r2_ref, e2_ref = refs[n_cast:n_cast + 4]
    _cast_slabs(refs[:n_cast], refs[n_cast + 4:])
    lanes = q_ref.shape[0]
    key_row = lax.broadcasted_iota(jnp.int32, (N_KEYS, lanes), 0)
    iota8 = lax.broadcasted_iota(jnp.int32, (_GRID_HALF, lanes), 0)
    grid_pos = _candidate_pos(iota8)
    nt = (((1,), (1,)), ((), ()))
    for h0 in range(0, PEER_HEADS, group):
        heads = range(h0, h0 + group)
        scores = []
        for h in heads:
            for p, e_ref in ((0, e1_ref), (1, e2_ref)):
                c0 = (h * 2 + p) * PEER_HALF
                qhp = q_ref[:, c0:c0 + PEER_HALF].astype(BF16)
                s = lax.dot_general(keys_ref[h, p], qhp, nt, preferred_element_type=F32)
                e_ref[h] = jnp.exp(s - jnp.max(s, axis=0, keepdims=True)).astype(e_ref.dtype)
                scores.append(s)
        found = _top16(scores, key_row, N_KEYS)
        grids = []
        for i, h in enumerate(heads):
            (_, v1, _), (_, v2, picks2) = found[2 * i], found[2 * i + 1]
            r2 = jnp.full((N_KEYS, lanes), float(PEER_TOPK), F32)
            for k in range(PEER_TOPK):
                r2 = jnp.where(key_row == picks2[k:k + 1, :], float(k), r2)
            r2_ref[h] = r2.astype(r2_ref.dtype)
            grids.append(_candidate_grid(v1, v2, iota8))
        best = _top16(grids, grid_pos, PEER_TOPK * PEER_TOPK)
        for i, h in enumerate(heads):
            left, vals, _ = best[i]
            z = jnp.sum(jnp.exp(vals - vals[0:1, :]), axis=0, keepdims=True)
            cnt_a = _kept_counts(left, iota8)
            picks1 = found[2 * i][2]
            cnt = jnp.zeros((N_KEYS, lanes), F32)
            for k in range(PEER_TOPK):
                cnt = jnp.where(key_row == picks1[k:k + 1, :], cnt_a[k], cnt)
            cnt_ref[h] = cnt
            e1_ref[h] = e1_ref[h] * (1.0 / z)


def _peer_select(q, sub_keys, *, tq, to_bf16=()):
    t = q.shape[0]
    tq = min(tq, t)
    n_steps = t // tq
    tab = jax.ShapeDtypeStruct((PEER_HEADS, N_KEYS, t), F32)
    tab_bf = jax.ShapeDtypeStruct((PEER_HEADS, N_KEYS, t), BF16)
    tab_spec = pl.BlockSpec((PEER_HEADS, N_KEYS, tq), lambda i: (0, 0, i))
    slab_specs, slab_shapes = _slab_specs(to_bf16, n_steps, lambda i: (i, 0))
    res = pl.pallas_call(
        functools.partial(_peer_select_kernel, group=SELECT_GROUP, n_cast=len(to_bf16)),
        grid=(n_steps,),
        in_specs=[
            pl.BlockSpec((tq, q.shape[1]), lambda i: (i, 0)),
            pl.BlockSpec(sub_keys.shape, lambda i: (0, 0, 0, 0)),
        ] + slab_specs,
        out_specs=[tab_spec] * 4 + slab_specs,
        out_shape=[tab, tab, tab_bf, tab_bf] + slab_shapes,
        compiler_params=_cparams(("arbitrary",)),
        name="peer_select_cast" if to_bf16 else "peer_select",
    )(q, sub_keys, *to_bf16)
    return res[:4], res[4:]


def _peer_ffn_kernel(hnt_ref, u_ref, v_ref, cnt_ref, e1_ref, r2_ref, e2_ref, x1_ref, fg_ref, o_ref, act_ref, a_ref,
                     *, rows_per_tile, n_exp_tiles):
    s = pl.program_id(0)
    jp = jnp.maximum(s - 1, 0) % n_exp_tiles

    @pl.when(s == 0)
    def _():
        act_ref[...] = jnp.zeros_like(act_ref)

    @pl.when(jp == 0)
    def _():
        o_ref[...] = jnp.zeros_like(o_ref)

    tm = act_ref.shape[1]

    def finish_rows(r):
        row = (jp * rows_per_tile) % SUBLANES + r
        gate = jnp.zeros((N_KEYS, tm), BF16)
        for h in range(PEER_HEADS):
            cnt = cnt_ref[h, pl.ds(row, 1), :].astype(BF16)
            e1 = e1_ref[h, pl.ds(row, 1), :].astype(BF16)
            gate = gate + jnp.where(r2_ref[h] < cnt, e2_ref[h], jnp.zeros((), BF16)) * e1
        x = act_ref[r * N_KEYS:(r + 1) * N_KEYS, :]
        gelu = 0.5 * x * (1.0 + lax.erf(x * (2.0 ** -0.5)))
        a_ref[r * N_KEYS:(r + 1) * N_KEYS, :] = (gate.astype(F32) * gelu).astype(BF16)

    act_new = jnp.dot(u_ref[...], hnt_ref[...], preferred_element_type=F32)
    for r in range(rows_per_tile):
        finish_rows(r)
    tn = (((0,), (0,)), ((), ()))
    o_ref[...] += lax.dot_general(a_ref[...], v_ref[...], tn, preferred_element_type=F32)
    act_ref[...] = act_new

    @pl.when(jp == n_exp_tiles - 1)
    def _():
        def step(r, carry):
            rows = pl.ds(pl.multiple_of(r * NORM_ROWS, NORM_ROWS), NORM_ROWS)
            o_ref[rows, :] = _rms(x1_ref[rows, :] + o_ref[rows, :], fg_ref[...])
            return carry

        lax.fori_loop(0, tm // NORM_ROWS, step, 0)


def _peer_ffn(hn_t, u_bf, v_bf, tables, x1, final_g, *, tm, te):
    d, t = hn_t.shape
    n_exp = u_bf.shape[0]
    tm = min(tm, t)
    rows_per_tile = te // N_KEYS
    assert SUBLANES % rows_per_tile == 0 and t % tm == 0 and n_exp % te == 0
    ni, nj = t // tm, n_exp // te
    last = ni * nj - 1

    def cur(s):
        s = jnp.minimum(s, last)
        return s // nj, s % nj

    def prev(s):
        s = jnp.maximum(s - 1, 0)
        return s // nj, s % nj

    tab_spec = pl.BlockSpec((PEER_HEADS, N_KEYS, tm), lambda s: (0, 0, prev(s)[0]))
    row_spec = pl.BlockSpec((PEER_HEADS, SUBLANES, tm),
                            lambda s: (0, (prev(s)[1] * rows_per_tile) // SUBLANES, prev(s)[0]))
    return pl.pallas_call(
        functools.partial(_peer_ffn_kernel, rows_per_tile=rows_per_tile, n_exp_tiles=nj),
        grid=(ni * nj + 1,),
        in_specs=[
            pl.BlockSpec((d, tm), lambda s: (0, cur(s)[0])),
            pl.BlockSpec((te, d), lambda s: (cur(s)[1], 0)),
            pl.BlockSpec((te, d), lambda s: (prev(s)[1], 0)),
            row_spec, row_spec, tab_spec, tab_spec,
            _row_tile_spec((tm, d), lambda s: (prev(s)[0], 0), True),
            pl.BlockSpec((1, d), lambda s: (0, 0)),
        ],
        out_specs=pl.BlockSpec((tm, d), lambda s: (prev(s)[0], 0)),
        out_shape=jax.ShapeDtypeStruct((t, d), F32),
        scratch_shapes=[pltpu.VMEM((te, tm), F32), pltpu.VMEM((te, tm), BF16)],
        compiler_params=_cparams(("arbitrary",)),
        name="peer_ffn",
    )(hn_t, u_bf, v_bf, *tables, x1, final_g)


def _group(x, hist, cache_kv, p, *, n_streams):
    c = p["w_dw"].shape[1]
    a = p["g_attn"].shape[1]
    n_heads = a // HEAD_DIM
    tl = TILES
    z = _norm_matmul(x, p["mix_g"], p["w_in"], p["b_in"], tm=tl["proj_rows"], tn=tl["proj_cols"],
                     single_buffer=True)
    pending = (p["w_out"], p["w_q"]) if p["w_out"].dtype != BF16 else ()
    u, conv_n, done = _conv_module(z, hist, p["w_dw"], p["b_dw"], p["ln_g"], p["ln_b"], p["g_conv"],
                                   n_streams=n_streams, tt=tl["conv_rows"], to_bf16=pending)
    if done:
        p = dict(p, w_out=done[0], w_q=done[1])
    if cache_kv is None:
        blk = 2 * c // HEAD_DIM
        attn = _attn_prompt(z, p["frow"], n_heads=n_heads, q_blk=blk, k_blk=blk + n_heads, v_blk=blk + 2 * n_heads,
                            tq=tl["attn_rows"])
    else:
        blk = 2 * c // a
        attn = _attn_sample(z, cache_kv[0], cache_kv[1], p["frow"], n_streams=n_streams, n_heads=n_heads,
                            q_blk=blk, k_blk=blk + 1, v_blk=blk + 2)
    x1 = _merge(conv_n, attn, p["g_attn"], p["w_out"], x, tm=tl["proj_rows"], tn=tl["proj_cols"])
    q, hn_t = _norm_matmul(x1, p["ffn_g"], p["w_q"], p["zero_bq"], tm=tl["proj_rows"], tn=tl["proj_cols"],
                         emit_norm=True, single_buffer=True)
    pending = (p["peer_u"], p["peer_v"]) if p["peer_u"].dtype != BF16 else ()
    tables, done = _peer_select(q, p["sub_keys"], tq=tl["select_tokens"], to_bf16=pending)
    if done:
        p = dict(p, peer_u=done[0], peer_v=done[1])
    y = _peer_ffn(hn_t, p["peer_u"], p["peer_v"], tables, x1, p["final_g"], tm=tl["ffn_tokens"], te=tl["ffn_experts"])
    return y, z, u, p


def kernel(x_prompt, x_sample, cache_conv, cache_k, cache_v, mix_norm_g, w_in, b_in, w_dw, b_dw, ln_g, ln_b,
           rel_bias, out_norm_conv_g, out_norm_attn_g, w_out, ffn_norm_g, peer_w_q, peer_sub_keys, peer_u, peer_v,
           final_norm_g):
    assert mix_norm_g.shape[0] == 1, "single-layer step"
    bsz, seq, d = x_prompt.shape
    db, ds, _ = x_sample.shape
    c = w_dw.shape[2]
    a = out_norm_attn_g.shape[1]
    n_heads = a // HEAD_DIM
    assert bsz == 1 and seq % ATTN_Q_TILE == 0 and ds >= CONV_WIDTH - 1

    row = lambda v: v.reshape(1, -1)
    p = dict(
        mix_g=row(mix_norm_g[0]), w_in=w_in[0], b_in=row(b_in[0]),
        w_dw=w_dw[0], b_dw=row(b_dw[0]), ln_g=row(ln_g[0]), ln_b=row(ln_b[0]),
        g_conv=row(out_norm_conv_g[0]), g_attn=row(out_norm_attn_g[0]), w_out=w_out[0],
        ffn_g=row(ffn_norm_g[0]), w_q=peer_w_q[0],
        zero_bq=jnp.zeros((1, peer_w_q.shape[2]), F32),
        sub_keys=peer_sub_keys[0].astype(BF16), peer_u=peer_u[0], peer_v=peer_v[0],
        final_g=row(final_norm_g),
        frow=rel_bias[0][:, _bias_row_index()].reshape(n_heads, 1, TOEPLITZ_N),
    )

    hist_p = jnp.zeros((1, CONV_HALO, c), F32)
    y_p, z_p, u_p, p = _group(x_prompt.reshape(seq, d), hist_p, None, p, n_streams=1)

    pad = CONV_HALO - (CONV_WIDTH - 1)
    hist_s = jnp.pad(cache_conv[0], ((0, 0), (pad, 0), (0, 0)))
    r_len = cache_k.shape[2]
    cache_kv = (cache_k[0].reshape(db, r_len * n_heads, HEAD_DIM), cache_v[0].reshape(db, r_len * n_heads, HEAD_DIM))
    y_s, z_s, u_s, _ = _group(x_sample.reshape(db * ds, d), hist_s, cache_kv, p, n_streams=db)

    k0, v0 = 2 * c + a, 2 * c + 2 * a
    rows_p = min(BAND_ROWS, seq)
    keep = CONV_WIDTH - 1
    return (
        y_p.reshape(1, seq, d),
        y_s.reshape(db, ds, d),
        u_p[seq - keep:].reshape(1, 1, keep, c),
        z_p[seq - rows_p:, k0:k0 + a].reshape(1, 1, rows_p, n_heads, HEAD_DIM),
        z_p[seq - rows_p:, v0:v0 + a].reshape(1, 1, rows_p, n_heads, HEAD_DIM),
        u_s.reshape(db, ds, c)[:, ds - keep:].reshape(1, db, keep, c),
        z_s[:, k0:k0 + a].reshape(1, db, ds, n_heads, HEAD_DIM),
        z_s[:, v0:v0 + a].reshape(1, db, ds, n_heads, HEAD_DIM),
    )
```

```python
import functools

import numpy as np
import jax
import jax.numpy as jnp
from jax import lax
from jax.experimental import pallas as pl
from jax.experimental.pallas import tpu as pltpu

F32 = jnp.float32
BF16 = jnp.bfloat16

CHUNK = 64
CONV_WIDTH = 31
CONV_HALO = 32
HEAD_DIM = 128
N_PREV_CHUNKS = 8
BAND_ROWS = N_PREV_CHUNKS * CHUNK
REL_CLIP = 256
ATTN_SCALE = HEAD_DIM ** -0.5
PEER_HEADS = 8
PEER_HALF = 128
N_KEYS = 128
PEER_TOPK = 16
SUBLANES = 8
LANES = 128
MXU_WIDTH = 256
EPS = 1e-6
NEG_INF = -1e30

ATTN_Q_TILE = 4 * CHUNK
ATTN_WINDOW = BAND_ROWS + ATTN_Q_TILE
TOEPLITZ_N = 1024

VMEM_LIMIT = 56 * 1024 * 1024

TILES = dict(
    proj_rows=1024, proj_cols=512,
    conv_rows=128,
    attn_rows=1024,
    select_tokens=128,
    ffn_tokens=512, ffn_experts=512,
)


def _cparams(sem, flags=None):
    return pltpu.CompilerParams(dimension_semantics=sem, vmem_limit_bytes=VMEM_LIMIT, flags=flags)


def _rms(x, g):
    ms = jnp.mean(x * x, axis=-1, keepdims=True)
    return x * lax.rsqrt(ms + EPS) * g


def _slab_specs(mats, n_steps, index_map):
    specs, shapes = [], []
    for m in mats:
        rows = m.shape[0] // n_steps
        assert m.shape[0] % n_steps == 0 and rows % 16 == 0, (m.shape, n_steps)
        specs.append(pl.BlockSpec((rows, m.shape[1]), index_map))
        shapes.append(jax.ShapeDtypeStruct(m.shape, BF16))
    return specs, shapes


def _cast_slabs(src_refs, dst_refs):
    for src, dst in zip(src_refs, dst_refs, strict=True):
        dst[...] = src[...].astype(BF16)


NORM_ROWS = 64


def _norm_rows_to(dst_ref, n_rows, row_fn):
    def step(r, carry):
        rows = pl.ds(pl.multiple_of(r * NORM_ROWS, NORM_ROWS), NORM_ROWS)
        dst_ref[rows, :] = row_fn(rows).astype(BF16)
        return carry

    lax.fori_loop(0, n_rows // NORM_ROWS, step, 0)


def _norm_matmul_kernel(x_ref, g_ref, w_ref, b_ref, o_ref, *rest, emit_norm, emit_weight):
    xn_ref = rest[-1]

    @pl.when(pl.program_id(1) == 0)
    def _():
        _norm_rows_to(xn_ref, x_ref.shape[0], lambda rows: _rms(x_ref[rows, :], g_ref[...]))
        if emit_norm:
            def step(r, carry):
                rows = pl.ds(pl.multiple_of(r * LANES, LANES), LANES)
                rest[0][:, rows] = xn_ref[rows, :].astype(F32).T.astype(BF16)
                return carry

            lax.fori_loop(0, x_ref.shape[0] // LANES, step, 0)

    w = w_ref[...].astype(BF16)
    if emit_weight:
        rest[-2][...] = w
    o_ref[...] = jnp.dot(xn_ref[...], w, preferred_element_type=F32) + b_ref[...]


def _row_tile_spec(block_shape, index_map, single_buffer):
    if single_buffer:
        return pl.BlockSpec(block_shape, index_map, pipeline_mode=pl.Buffered(1))
    return pl.BlockSpec(block_shape, index_map)


def _norm_matmul(x, g, w, b, *, tm, tn, emit_norm=False, emit_weight=False, single_buffer=False):
    t, d = x.shape
    n = w.shape[1]
    tm = min(tm, t)
    assert t % tm == 0 and n % tn == 0 and tm % NORM_ROWS == 0
    out_shape = [jax.ShapeDtypeStruct((t, n), F32)]
    out_specs = [pl.BlockSpec((tm, tn), lambda i, j: (i, j))]
    scratch = [pltpu.VMEM((tm, d), BF16)]
    if emit_norm:
        assert tm % LANES == 0
        out_shape.append(jax.ShapeDtypeStruct((d, t), BF16))
        out_specs.append(pl.BlockSpec((d, tm), lambda i, j: (0, i)))
    if emit_weight:
        out_shape.append(jax.ShapeDtypeStruct((d, n), BF16))
        out_specs.append(pl.BlockSpec((d, tn), lambda i, j: (0, j)))
    res = pl.pallas_call(
        functools.partial(_norm_matmul_kernel, emit_norm=emit_norm, emit_weight=emit_weight),
        grid=(t // tm, n // tn),
        in_specs=[
            _row_tile_spec((tm, d), lambda i, j: (i, 0), single_buffer),
            pl.BlockSpec((1, d), lambda i, j: (0, 0)),
            pl.BlockSpec((d, tn), lambda i, j: (0, j)),
            pl.BlockSpec((1, tn), lambda i, j: (0, j)),
        ],
        out_specs=out_specs,
        out_shape=out_shape,
        scratch_shapes=scratch,
        compiler_params=_cparams(("parallel", "arbitrary")),
        name="norm_matmul_emit" if emit_norm else ("norm_matmul_wcopy" if emit_weight else "norm_matmul"),
    )(x, g, w, b)
    return res if len(res) > 1 else res[0]


def _conv_kernel(val_ref, gate_ref, hist_ref, w_ref, bdw_ref, lng_ref, lnb_ref, gc_ref, *refs, tt, n_cast):
    u_ref, o_ref = refs[n_cast:n_cast + 2]
    ush_ref, par_ref = refs[2 * n_cast + 2:]
    _cast_slabs(refs[:n_cast], refs[n_cast + 2:2 * n_cast + 2])
    t = pl.program_id(1)
    n_rows = CONV_HALO + tt
    n_ch = w_ref.shape[1]

    @pl.when((pl.program_id(0) == 0) & (t == 0))
    def _():
        for k in range(CONV_WIDTH):
            par_ref[k] = jnp.broadcast_to(w_ref[k:k + 1, :], (SUBLANES, n_ch))
        for k, ref in enumerate((bdw_ref, lng_ref, lnb_ref, gc_ref)):
            par_ref[CONV_WIDTH + k] = jnp.broadcast_to(ref[...], (SUBLANES, n_ch))

    @pl.when(t == 0)
    def _():
        ush_ref[0, 0:CONV_HALO, :] = hist_ref[0]

    @pl.when(t > 0)
    def _():
        ush_ref[0, 0:CONV_HALO, :] = ush_ref[0, tt:tt + CONV_HALO, :]

    u = val_ref[...] * jax.nn.sigmoid(gate_ref[...])
    u_ref[...] = u
    ush_ref[0, CONV_HALO:n_rows, :] = u
    upad = ush_ref[0]
    for s in range(1, SUBLANES):
        ush_ref[s] = pltpu.roll(upad, n_rows - s, 0)

    rows = SUBLANES
    for c in range(tt // rows):
        r0 = c * rows
        acc = par_ref[CONV_WIDTH]
        for k in range(CONV_WIDTH):
            off = CONV_HALO - (CONV_WIDTH - 1) + k
            lo = r0 + off - off % SUBLANES
            acc = acc + ush_ref[off % SUBLANES, lo:lo + rows, :] * par_ref[k]
        mu = jnp.mean(acc, axis=-1, keepdims=True)
        cen = acc - mu
        var = jnp.mean(cen * cen, axis=-1, keepdims=True)
        yn = cen * lax.rsqrt(var + EPS) * par_ref[CONV_WIDTH + 1] + par_ref[CONV_WIDTH + 2]
        s = yn * jax.nn.sigmoid(yn)
        o_ref[r0:r0 + rows, :] = _rms(s, par_ref[CONV_WIDTH + 3]).astype(BF16)


def _conv_module(z, hist, w_dw, b_dw, ln_g, ln_b, g_conv, *, n_streams, tt, to_bf16=()):
    t_all = z.shape[0]
    c = w_dw.shape[1]
    t_stream = t_all // n_streams
    tt = min(tt, t_stream)
    nt = t_stream // tt
    slab_specs, slab_shapes = _slab_specs(to_bf16, n_streams * nt, lambda b, t: (b * nt + t, 0))
    res = pl.pallas_call(
        functools.partial(_conv_kernel, tt=tt, n_cast=len(to_bf16)),
        grid=(n_streams, nt),
        in_specs=[
            pl.BlockSpec((tt, c), lambda b, t: (b * nt + t, 0)),
            pl.BlockSpec((tt, c), lambda b, t: (b * nt + t, 1)),
            pl.BlockSpec((1, CONV_HALO, c), lambda b, t: (b, 0, 0)),
            pl.BlockSpec((CONV_WIDTH, c), lambda b, t: (0, 0)),
            pl.BlockSpec((1, c), lambda b, t: (0, 0)),
            pl.BlockSpec((1, c), lambda b, t: (0, 0)),
            pl.BlockSpec((1, c), lambda b, t: (0, 0)),
            pl.BlockSpec((1, c), lambda b, t: (0, 0)),
        ] + slab_specs,
        out_specs=[
            pl.BlockSpec((tt, c), lambda b, t: (b * nt + t, 0)),
            pl.BlockSpec((tt, c), lambda b, t: (b * nt + t, 0)),
        ] + slab_specs,
        out_shape=[jax.ShapeDtypeStruct((t_all, c), F32), jax.ShapeDtypeStruct((t_all, c), BF16)] + slab_shapes,
        scratch_shapes=[pltpu.VMEM((SUBLANES, CONV_HALO + tt, c), F32),
                        pltpu.VMEM((CONV_WIDTH + 4, SUBLANES, c), F32)],
        compiler_params=_cparams(("arbitrary", "arbitrary")),
        name="conv_module_cast" if to_bf16 else "conv_module",
    )(z, z, hist, w_dw, b_dw, ln_g, ln_b, g_conv, *to_bf16)
    return res[0], res[1], res[2:]


def _bias_row_index():
    m = np.arange(TOEPLITZ_N)
    dist = np.where(m <= TOEPLITZ_N - ATTN_Q_TILE, BAND_ROWS - m, BAND_ROWS + TOEPLITZ_N - m)
    return np.clip(dist, -REL_CLIP, REL_CLIP) + REL_CLIP


def _toeplitz(frow, rows):
    x = jnp.broadcast_to(frow, (rows, frow.shape[-1]))
    return pltpu.roll(x, 0, 1, stride=1, stride_axis=0)


def _attn_prompt_kernel(q_ref, k_ref, v_ref, f_ref, o_ref, kbuf_ref, vbuf_ref, bm_ref, *, t):
    i = pl.program_id(1)

    @pl.when(i == 0)
    def _():
        zeros = jnp.zeros((BAND_ROWS, HEAD_DIM), BF16)
        kbuf_ref[0:BAND_ROWS, :] = zeros
        vbuf_ref[0:BAND_ROWS, :] = zeros
        kbuf_ref[BAND_ROWS:BAND_ROWS + t, :] = k_ref[...].astype(BF16)
        vbuf_ref[BAND_ROWS:BAND_ROWS + t, :] = v_ref[...].astype(BF16)
        bias = _toeplitz(f_ref[0], ATTN_Q_TILE)[:, :ATTN_WINDOW]
        qc = lax.broadcasted_iota(jnp.int32, (ATTN_Q_TILE, ATTN_WINDOW), 0) // CHUNK
        kc = lax.broadcasted_iota(jnp.int32, (ATTN_Q_TILE, ATTN_WINDOW), 1) // CHUNK
        band = (kc >= qc) & (kc <= qc + N_PREV_CHUNKS)
        bm_ref[...] = jnp.where(band, bias, NEG_INF)

    col = lax.broadcasted_iota(jnp.int32, (ATTN_Q_TILE, ATTN_WINDOW), 1)
    for j in range(q_ref.shape[0] // ATTN_Q_TILE):
        rows = slice(j * ATTN_Q_TILE, (j + 1) * ATTN_Q_TILE)
        start = pl.multiple_of(i * q_ref.shape[0] + j * ATTN_Q_TILE, ATTN_Q_TILE)
        q = (q_ref[rows, :] * ATTN_SCALE).astype(BF16)
        kw = kbuf_ref[pl.ds(start, ATTN_WINDOW), :]
        vw = vbuf_ref[pl.ds(start, ATTN_WINDOW), :]
        s = lax.dot_general(q, kw, (((1,), (1,)), ((), ())), preferred_element_type=F32) + bm_ref[...]
        s = jnp.where(col >= BAND_ROWS - start, s, NEG_INF)
        m = jnp.max(s, axis=-1, keepdims=True)
        p = jnp.exp(s - m)
        l = jnp.sum(p, axis=-1, keepdims=True)
        o = jnp.dot(p.astype(BF16), vw, preferred_element_type=F32)
        o_ref[rows, :] = o / l


def _attn_prompt(z, frow, *, n_heads, q_blk, k_blk, v_blk, tq):
    t = z.shape[0]
    tq = min(tq, t)
    assert t % tq == 0 and tq % ATTN_Q_TILE == 0
    return pl.pallas_call(
        functools.partial(_attn_prompt_kernel, t=t),
        grid=(n_heads, t // tq),
        in_specs=[
            pl.BlockSpec((tq, HEAD_DIM), lambda h, i: (i, q_blk + h)),
            pl.BlockSpec((t, HEAD_DIM), lambda h, i: (0, k_blk + h)),
            pl.BlockSpec((t, HEAD_DIM), lambda h, i: (0, v_blk + h)),
            pl.BlockSpec((1, 1, TOEPLITZ_N), lambda h, i: (h, 0, 0)),
        ],
        out_specs=pl.BlockSpec((tq, HEAD_DIM), lambda h, i: (i, h)),
        out_shape=jax.ShapeDtypeStruct((t, n_heads * HEAD_DIM), F32),
        scratch_shapes=[
            pltpu.VMEM((BAND_ROWS + t, HEAD_DIM), BF16),
            pltpu.VMEM((BAND_ROWS + t, HEAD_DIM), BF16),
            pltpu.VMEM((ATTN_Q_TILE, ATTN_WINDOW), F32),
        ],
        compiler_params=_cparams(("arbitrary", "arbitrary")),
        name="attn_prompt",
    )(z, z, z, frow)


def _attn_sample_kernel(q_ref, k_ref, v_ref, ck_ref, cv_ref, f_ref, o_ref, bias_ref, *, n_heads, s_len, r_len):
    @pl.when(pl.program_id(0) == 0)
    def _():
        for h in range(n_heads):
            bias_ref[h] = _toeplitz(f_ref[h], s_len)

    for h in range(n_heads):
        hs = slice(h * HEAD_DIM, (h + 1) * HEAD_DIM)
        q = (q_ref[:, hs] * ATTN_SCALE).astype(BF16)
        kc = ck_ref[0, pl.ds(h, r_len, stride=n_heads), :].astype(BF16)
        vc = cv_ref[0, pl.ds(h, r_len, stride=n_heads), :].astype(BF16)
        kn = k_ref[:, hs].astype(BF16)
        vn = v_ref[:, hs].astype(BF16)
        nt = (((1,), (1,)), ((), ()))
        off = BAND_ROWS - r_len
        s_c = lax.dot_general(q, kc, nt, preferred_element_type=F32) + bias_ref[h, :, off:off + r_len]
        s_n = lax.dot_general(q, kn, nt, preferred_element_type=F32) + bias_ref[h, :, BAND_ROWS:BAND_ROWS + s_len]
        m = jnp.maximum(jnp.max(s_c, axis=-1, keepdims=True), jnp.max(s_n, axis=-1, keepdims=True))
        p_c = jnp.exp(s_c - m)
        p_n = jnp.exp(s_n - m)
        l = jnp.sum(p_c, axis=-1, keepdims=True) + jnp.sum(p_n, axis=-1, keepdims=True)
        o = (jnp.dot(p_c.astype(BF16), vc, preferred_element_type=F32)
             + jnp.dot(p_n.astype(BF16), vn, preferred_element_type=F32))
        o_ref[:, hs] = o / l


def _attn_sample(z, cache_k, cache_v, frow, *, n_streams, n_heads, q_blk, k_blk, v_blk):
    s_len = z.shape[0] // n_streams
    r_len = cache_k.shape[1] // n_heads
    a = n_heads * HEAD_DIM
    return pl.pallas_call(
        functools.partial(_attn_sample_kernel, n_heads=n_heads, s_len=s_len, r_len=r_len),
        grid=(n_streams,),
        in_specs=[
            pl.BlockSpec((s_len, a), lambda b: (b, q_blk)),
            pl.BlockSpec((s_len, a), lambda b: (b, k_blk)),
            pl.BlockSpec((s_len, a), lambda b: (b, v_blk)),
            pl.BlockSpec((1, r_len * n_heads, HEAD_DIM), lambda b: (b, 0, 0)),
            pl.BlockSpec((1, r_len * n_heads, HEAD_DIM), lambda b: (b, 0, 0)),
            pl.BlockSpec((n_heads, 1, TOEPLITZ_N), lambda b: (0, 0, 0)),
        ],
        out_specs=pl.BlockSpec((s_len, a), lambda b: (b, 0)),
        out_shape=jax.ShapeDtypeStruct((z.shape[0], a), F32),
        scratch_shapes=[pltpu.VMEM((n_heads, s_len, TOEPLITZ_N), F32)],
        compiler_params=_cparams(("arbitrary",)),
        name="attn_sample",
    )(z, z, z, cache_k, cache_v, frow)


def _merge_kernel(cn_ref, at_ref, ga_ref, w_ref, x_ref, o_ref, an_ref):
    c = cn_ref.shape[1]

    @pl.when(pl.program_id(1) == 0)
    def _():
        _norm_rows_to(an_ref, at_ref.shape[0], lambda rows: _rms(at_ref[rows, :], ga_ref[...]))

    o_ref[...] = (x_ref[...]
                  + jnp.dot(cn_ref[...], w_ref[0:c, :].astype(BF16), preferred_element_type=F32)
                  + jnp.dot(an_ref[...], w_ref[c:, :].astype(BF16), preferred_element_type=F32))


def _merge(conv_n, attn, g_attn, w_out, x, *, tm, tn):
    t, d = x.shape
    c = conv_n.shape[1]
    a = attn.shape[1]
    tm = min(tm, t)
    assert t % tm == 0 and d % tn == 0 and tm % NORM_ROWS == 0
    return pl.pallas_call(
        _merge_kernel,
        grid=(t // tm, d // tn),
        in_specs=[
            pl.BlockSpec((tm, c), lambda i, j: (i, 0)),
            pl.BlockSpec((tm, a), lambda i, j: (i, 0)),
            pl.BlockSpec((1, a), lambda i, j: (0, 0)),
            pl.BlockSpec((c + a, tn), lambda i, j: (0, j)),
            pl.BlockSpec((tm, tn), lambda i, j: (i, j)),
        ],
        out_specs=pl.BlockSpec((tm, tn), lambda i, j: (i, j)),
        out_shape=jax.ShapeDtypeStruct((t, d), F32),
        scratch_shapes=[pltpu.VMEM((tm, a), BF16)],
        compiler_params=_cparams(("parallel", "arbitrary")),
        name="merge",
    )(conv_n, attn, g_attn, w_out, x)


def _top16(scores, pos, n_pos):
    lanes = pos.shape[1]
    slot = lax.broadcasted_iota(jnp.int32, (PEER_TOPK, lanes), 0)

    def body(k, carry):
        out = []
        for s, vals, picks in carry:
            m = jnp.max(s, axis=0, keepdims=True)
            idx = jnp.min(jnp.where(s == m, pos, n_pos), axis=0, keepdims=True)
            s = jnp.where(pos == idx, -jnp.inf, s)
            vals = jnp.where(slot == k, m, vals)
            picks = jnp.where(slot == k, idx, picks)
            out.append((s, vals, picks))
        return tuple(out)

    init = tuple((s, jnp.zeros((PEER_TOPK, lanes), F32), jnp.zeros((PEER_TOPK, lanes), jnp.int32)) for s in scores)
    return lax.fori_loop(0, PEER_TOPK, body, init)


_GRID_HALF = PEER_TOPK // 2
SELECT_GROUP = 2


def _candidate_grid(v1, v2, iota8):
    neg = -jnp.inf
    rows = [v1[0:1, :] + v2]
    for a in range(1, _GRID_HALF):
        rows.append(jnp.where(iota8 < PEER_TOPK // (a + 1), v1[a:a + 1, :] + v2[0:_GRID_HALF, :], neg))
    rows.append(v1[_GRID_HALF:, :] + v2[0:1, :])
    return jnp.concatenate(rows, axis=0)


def _candidate_pos(iota8):
    pos = [iota8, iota8 + _GRID_HALF]
    pos += [iota8 + a * PEER_TOPK for a in range(1, _GRID_HALF)]
    pos.append((iota8 + _GRID_HALF) * PEER_TOPK)
    return jnp.concatenate(pos, axis=0)


def _kept_counts(left, iota8):
    gone = (left == -jnp.inf).astype(F32)
    cnt = [jnp.sum(gone[0:PEER_TOPK, :], axis=0, keepdims=True)]
    for a in range(1, _GRID_HALF):
        r0 = PEER_TOPK + (a - 1) * _GRID_HALF
        piece = jnp.where(iota8 < PEER_TOPK // (a + 1), gone[r0:r0 + _GRID_HALF, :], 0.0)
        cnt.append(jnp.sum(piece, axis=0, keepdims=True))
    r0 = PEER_TOPK + (_GRID_HALF - 1) * _GRID_HALF
    cnt += [gone[r0 + i:r0 + i + 1, :] for i in range(_GRID_HALF)]
    return cnt


def _peer_select_kernel(q_ref, keys_ref, *refs, group, n_cast):
    cnt_ref, e1_ref, r2_ref, e2_ref = refs[n_cast:n_cast + 4]
    _cast_slabs(refs[:n_cast], refs[n_cast + 4:])
    lanes = q_ref.shape[0]
    key_row = lax.broadcasted_iota(jnp.int32, (N_KEYS, lanes), 0)
    iota8 = lax.broadcasted_iota(jnp.int32, (_GRID_HALF, lanes), 0)
    grid_pos = _candidate_pos(iota8)
    nt = (((1,), (1,)), ((), ()))
    for h0 in range(0, PEER_HEADS, group):
        heads = range(h0, h0 + group)
        scores = []
        for h in heads:
            for p, e_ref in ((0, e1_ref), (1, e2_ref)):
                c0 = (h * 2 + p) * PEER_HALF
                qhp = q_ref[:, c0:c0 + PEER_HALF].astype(BF16)
                s = lax.dot_general(keys_ref[h, p], qhp, nt, preferred_element_type=F32)
                e_ref[h] = jnp.exp(s - jnp.max(s, axis=0, keepdims=True)).astype(e_ref.dtype)
                scores.append(s)
        found = _top16(scores, key_row, N_KEYS)
        grids = []
        for i, h in enumerate(heads):
            (_, v1, _), (_, v2, picks2) = found[2 * i], found[2 * i + 1]
            r2 = jnp.full((N_KEYS, lanes), float(PEER_TOPK), F32)
            for k in range(PEER_TOPK):
                r2 = jnp.where(key_row == picks2[k:k + 1, :], float(k), r2)
            r2_ref[h] = r2.astype(r2_ref.dtype)
            grids.append(_candidate_grid(v1, v2, iota8))
        best = _top16(grids, grid_pos, PEER_TOPK * PEER_TOPK)
        for i, h in enumerate(heads):
            left, vals, _ = best[i]
            z = jnp.sum(jnp.exp(vals - vals[0:1, :]), axis=0, keepdims=True)
            cnt_a = _kept_counts(left, iota8)
            picks1 = found[2 * i][2]
            cnt = jnp.zeros((N_KEYS, lanes), F32)
            for k in range(PEER_TOPK):
                cnt = jnp.where(key_row == picks1[k:k + 1, :], cnt_a[k], cnt)
            cnt_ref[h] = cnt
            e1_ref[h] = e1_ref[h] * (1.0 / z)


def _peer_select(q, sub_keys, *, tq, to_bf16=()):
    t = q.shape[0]
    tq = min(tq, t)
    n_steps = t // tq
    tab = jax.ShapeDtypeStruct((PEER_HEADS, N_KEYS, t), F32)
    tab_bf = jax.ShapeDtypeStruct((PEER_HEADS, N_KEYS, t), BF16)
    tab_spec = pl.BlockSpec((PEER_HEADS, N_KEYS, tq), lambda i: (0, 0, i))
    slab_specs, slab_shapes = _slab_specs(to_bf16, n_steps, lambda i: (i, 0))
    res = pl.pallas_call(
        functools.partial(_peer_select_kernel, group=SELECT_GROUP, n_cast=len(to_bf16)),
        grid=(n_steps,),
        in_specs=[
            pl.BlockSpec((tq, q.shape[1]), lambda i: (i, 0)),
            pl.BlockSpec(sub_keys.shape, lambda i: (0, 0, 0, 0)),
        ] + slab_specs,
        out_specs=[tab_spec] * 4 + slab_specs,
        out_shape=[tab, tab, tab_bf, tab_bf] + slab_shapes,
        compiler_params=_cparams(("arbitrary",)),
        name="peer_select_cast" if to_bf16 else "peer_select",
    )(q, sub_keys, *to_bf16)
    return res[:4], res[4:]


def _peer_ffn_kernel(hnt_ref, u_ref, v_ref, cnt_ref, e1_ref, r2_ref, e2_ref, x1_ref, fg_ref, o_ref, act_ref, a_ref,
                     *, rows_per_tile, n_exp_tiles):
    s = pl.program_id(0)
    jp = jnp.maximum(s - 1, 0) % n_exp_tiles

    @pl.when(s == 0)
    def _():
        act_ref[...] = jnp.zeros_like(act_ref)

    @pl.when(jp == 0)
    def _():
        o_ref[...] = jnp.zeros_like(o_ref)

    tm = act_ref.shape[1]

    def finish_rows(r):
        row = (jp * rows_per_tile) % SUBLANES + r
        gate = jnp.zeros((N_KEYS, tm), BF16)
        for h in range(PEER_HEADS):
            cnt = cnt_ref[h, pl.ds(row, 1), :].astype(BF16)
            e1 = e1_ref[h, pl.ds(row, 1), :].astype(BF16)
            gate = gate + jnp.where(r2_ref[h] < cnt, e2_ref[h], jnp.zeros((), BF16)) * e1
        x = act_ref[r * N_KEYS:(r + 1) * N_KEYS, :]
        gelu = 0.5 * x * (1.0 + lax.erf(x * (2.0 ** -0.5)))
        a_ref[r * N_KEYS:(r + 1) * N_KEYS, :] = (gate.astype(F32) * gelu).astype(BF16)

    act_new = jnp.dot(u_ref[...], hnt_ref[...], preferred_element_type=F32)
    for r in range(rows_per_tile):
        finish_rows(r)
    tn = (((0,), (0,)), ((), ()))
    o_ref[...] += lax.dot_general(a_ref[...], v_ref[...], tn, preferred_element_type=F32)
    act_ref[...] = act_new

    @pl.when(jp == n_exp_tiles - 1)
    def _():
        def step(r, carry):
            rows = pl.ds(pl.multiple_of(r * NORM_ROWS, NORM_ROWS), NORM_ROWS)
            o_ref[rows, :] = _rms(x1_ref[rows, :] + o_ref[rows, :], fg_ref[...])
            return carry

        lax.fori_loop(0, tm // NORM_ROWS, step, 0)


def _peer_ffn(hn_t, u_bf, v_bf, tables, x1, final_g, *, tm, te):
    d, t = hn_t.shape
    n_exp = u_bf.shape[0]
    tm = min(tm, t)
    rows_per_tile = te // N_KEYS
    assert SUBLANES % rows_per_tile == 0 and t % tm == 0 and n_exp % te == 0
    ni, nj = t // tm, n_exp // te
    last = ni * nj - 1

    def cur(s):
        s = jnp.minimum(s, last)
        return s // nj, s % nj

    def prev(s):
        s = jnp.maximum(s - 1, 0)
        return s // nj, s % nj

    tab_spec = pl.BlockSpec((PEER_HEADS, N_KEYS, tm), lambda s: (0, 0, prev(s)[0]))
    row_spec = pl.BlockSpec((PEER_HEADS, SUBLANES, tm),
                            lambda s: (0, (prev(s)[1] * rows_per_tile) // SUBLANES, prev(s)[0]))
    return pl.pallas_call(
        functools.partial(_peer_ffn_kernel, rows_per_tile=rows_per_tile, n_exp_tiles=nj),
        grid=(ni * nj + 1,),
        in_specs=[
            pl.BlockSpec((d, tm), lambda s: (0, cur(s)[0])),
            pl.BlockSpec((te, d), lambda s: (cur(s)[1], 0)),
            pl.BlockSpec((te, d), lambda s: (prev(s)[1], 0)),
            row_spec, row_spec, tab_spec, tab_spec,
            _row_tile_spec((tm, d), lambda s: (prev(s)[0], 0), True),
            pl.BlockSpec((1, d), lambda s: (0, 0)),
        ],
        out_specs=pl.BlockSpec((tm, d), lambda s: (prev(s)[0], 0)),
        out_shape=jax.ShapeDtypeStruct((t, d), F32),
        scratch_shapes=[pltpu.VMEM((te, tm), F32), pltpu.VMEM((te, tm), BF16)],
        compiler_params=_cparams(("arbitrary",)),
        name="peer_ffn",
    )(hn_t, u_bf, v_bf, *tables, x1, final_g)


def _in_proj(x, p, *, emit_weight=False):
    return _norm_matmul(x, p["mix_g"], p["w_in"], p["b_in"], tm=TILES["proj_rows"], tn=TILES["proj_cols"],
                        emit_weight=emit_weight, single_buffer=True)


def _group(x, z, hist, cache_kv, p, *, n_streams):
    c = p["w_dw"].shape[1]
    a = p["g_attn"].shape[1]
    n_heads = a // HEAD_DIM
    tl = TILES
    pending =(p["w_out"], p["w_q"]) if p["w_out"].dtype != BF16 else ()
    u, conv_n, done = _conv_module(z, hist, p["w_dw"], p["b_dw"], p["ln_g"], p["ln_b"], p["g_conv"],
                                   n_streams=n_streams, tt=tl["conv_rows"], to_bf16=pending)
    if done:
        p = dict(p, w_out=done[0], w_q=done[1])
    if cache_kv is None:
        blk = 2 * c // HEAD_DIM
        attn = _attn_prompt(z, p["frow"], n_heads=n_heads, q_blk=blk, k_blk=blk + n_heads, v_blk=blk + 2 * n_heads,
                            tq=tl["attn_rows"])
    else:
        blk = 2 * c // a
        attn = _attn_sample(z, cache_kv[0], cache_kv[1], p["frow"], n_streams=n_streams, n_heads=n_heads,
                            q_blk=blk, k_blk=blk + 1, v_blk=blk + 2)
    x1 = _merge(conv_n, attn, p["g_attn"], p["w_out"], x, tm=tl["proj_rows"], tn=tl["proj_cols"])
    q, hn_t = _norm_matmul(x1, p["ffn_g"], p["w_q"], p["zero_bq"], tm=tl["proj_rows"], tn=tl["proj_cols"],
                         emit_norm=True, single_buffer=True)
    pending = (p["peer_u"], p["peer_v"]) if p["peer_u"].dtype != BF16 else ()
    tables, done = _peer_select(q, p["sub_keys"], tq=tl["select_tokens"], to_bf16=pending)
    if done:
        p = dict(p, peer_u=done[0], peer_v=done[1])
    y = _peer_ffn(hn_t, p["peer_u"], p["peer_v"], tables, x1, p["final_g"], tm=tl["ffn_tokens"], te=tl["ffn_experts"])
    return y, z, u, p


def kernel(x_prompt, x_sample, cache_conv, cache_k, cache_v, mix_norm_g, w_in, b_in, w_dw, b_dw, ln_g, ln_b,
           rel_bias, out_norm_conv_g, out_norm_attn_g, w_out, ffn_norm_g, peer_w_q, peer_sub_keys, peer_u, peer_v,
           final_norm_g):
    assert mix_norm_g.shape[0] == 1, "single-layer step"
    bsz, seq, d = x_prompt.shape
    db, ds, _ = x_sample.shape
    c = w_dw.shape[2]
    a = out_norm_attn_g.shape[1]
    n_heads = a // HEAD_DIM
    assert bsz == 1 and seq % ATTN_Q_TILE == 0 and ds >= CONV_WIDTH - 1

    row = lambda v: v.reshape(1, -1)
    p = dict(
        mix_g=row(mix_norm_g[0]), w_in=w_in[0], b_in=row(b_in[0]),
        w_dw=w_dw[0], b_dw=row(b_dw[0]), ln_g=row(ln_g[0]), ln_b=row(ln_b[0]),
        g_conv=row(out_norm_conv_g[0]), g_attn=row(out_norm_attn_g[0]), w_out=w_out[0],
        ffn_g=row(ffn_norm_g[0]), w_q=peer_w_q[0],
        zero_bq=jnp.zeros((1, peer_w_q.shape[2]), F32),
        sub_keys=peer_sub_keys[0].astype(BF16), peer_u=peer_u[0], peer_v=peer_v[0],
        final_g=row(final_norm_g),
        frow=rel_bias[0][:, _bias_row_index()].reshape(n_heads, 1, TOEPLITZ_N),
    )

    hist_p = jnp.zeros((1, CONV_HALO, c), F32)
    xs = x_sample.reshape(db * ds, d)
    z_s, w_in_bf = _in_proj(xs, p, emit_weight=True)
    p = dict(p, w_in=w_in_bf)
    xp = x_prompt.reshape(seq, d)
    y_p, z_p, u_p, p = _group(xp, _in_proj(xp, p), hist_p, None, p, n_streams=1)

    pad = CONV_HALO - (CONV_WIDTH - 1)
    hist_s = jnp.pad(cache_conv[0], ((0, 0), (pad, 0), (0, 0)))
    r_len = cache_k.shape[2]
    cache_kv = (cache_k[0].reshape(db, r_len * n_heads, HEAD_DIM), cache_v[0].reshape(db, r_len * n_heads, HEAD_DIM))
    y_s, _, u_s, _ = _group(xs, z_s, hist_s, cache_kv, p, n_streams=db)

    k0, v0 = 2 * c + a, 2 * c + 2 * a
    rows_p = min(BAND_ROWS, seq)
    keep = CONV_WIDTH - 1
    return (
        y_p.reshape(1, seq, d),
        y_s.reshape(db, ds, d),
        u_p[seq - keep:].reshape(1, 1, keep, c),
        z_p[seq - rows_p:, k0:k0 + a].reshape(1, 1, rows_p, n_heads, HEAD_DIM),
        z_p[seq - rows_p:, v0:v0 + a].reshape(1, 1, rows_p, n_heads, HEAD_DIM),
        u_s.reshape(db, ds, c)[:, ds - keep:].reshape(1, db, keep, c),
        z_s[:, k0:k0 + a].reshape(1, db, ds, n_heads, HEAD_DIM),
        z_s[:, v0:v0 + a].reshape(1, db, ds, n_heads, HEAD_DIM),
    )
```

```python
import functools

import numpy as np
import jax
import jax.numpy as jnp
from jax import lax
from jax.experimental import pallas as pl
from jax.experimental.pallas import tpu as pltpu

F32 = jnp.float32
BF16 = jnp.bfloat16

CHUNK = 64
CONV_WIDTH = 31
CONV_HALO = 32
HEAD_DIM = 128
N_PREV_CHUNKS = 8
BAND_ROWS = N_PREV_CHUNKS * CHUNK
REL_CLIP = 256
ATTN_SCALE = HEAD_DIM ** -0.5
PEER_HEADS = 8
PEER_HALF = 128
N_KEYS = 128
PEER_TOPK = 16
SUBLANES = 8
LANES = 128
MXU_WIDTH = 256
EPS = 1e-6
NEG_INF = -1e30

ATTN_Q_TILE = 4 * CHUNK
ATTN_WINDOW = BAND_ROWS + ATTN_Q_TILE
TOEPLITZ_N = 1024

VMEM_LIMIT = 56 * 1024 * 1024

TILES = dict(
    proj_rows=1024, proj_cols=512,
    conv_rows=128,
    attn_rows=2048,
    select_tokens=128,
    ffn_tokens=512, ffn_experts=512,
)


def _cparams(sem, flags=None):
    return pltpu.CompilerParams(dimension_semantics=sem, vmem_limit_bytes=VMEM_LIMIT, flags=flags)


def _rms(x, g):
    ms = jnp.mean(x * x, axis=-1, keepdims=True)
    return x * lax.rsqrt(ms + EPS) * g


def _slab_specs(mats, n_steps, index_map):
    specs, shapes = [], []
    for m in mats:
        rows = m.shape[0] // n_steps
        assert m.shape[0] % n_steps == 0 and rows % 16 == 0, (m.shape, n_steps)
        specs.append(pl.BlockSpec((rows, m.shape[1]), index_map))
        shapes.append(jax.ShapeDtypeStruct(m.shape, BF16))
    return specs, shapes


def _cast_slabs(src_refs, dst_refs):
    for src, dst in zip(src_refs, dst_refs, strict=True):
        dst[...] = src[...].astype(BF16)


NORM_ROWS = 64


def _norm_rows_to(dst_ref, n_rows, row_fn):
    def step(r, carry):
        rows = pl.ds(pl.multiple_of(r * NORM_ROWS, NORM_ROWS), NORM_ROWS)
        dst_ref[rows, :] = row_fn(rows).astype(BF16)
        return carry

    lax.fori_loop(0, n_rows // NORM_ROWS, step, 0)


def _norm_matmul_kernel(x_ref, g_ref, w_ref, b_ref, o_ref, *rest):
    xn_ref = rest[-1]

    @pl.when(pl.program_id(1) == 0)
    def _():
        _norm_rows_to(xn_ref, x_ref.shape[0], lambda rows: _rms(x_ref[rows, :], g_ref[...]))
        if len(rest) == 2:
            def step(r, carry):
                rows = pl.ds(pl.multiple_of(r * LANES, LANES), LANES)
                rest[0][:, rows] = xn_ref[rows, :].astype(F32).T.astype(BF16)
                return carry

            lax.fori_loop(0, x_ref.shape[0] // LANES, step, 0)

    o_ref[...] = jnp.dot(xn_ref[...], w_ref[...].astype(BF16), preferred_element_type=F32) + b_ref[...]


def _row_tile_spec(block_shape, index_map, single_buffer):
    if single_buffer:
        return pl.BlockSpec(block_shape, index_map, pipeline_mode=pl.Buffered(1))
    return pl.BlockSpec(block_shape, index_map)


def _norm_matmul(x, g, w, b, *, tm, tn, emit_norm=False, single_buffer=False):
    t, d = x.shape
    n = w.shape[1]
    tm = min(tm, t)
    assert t % tm == 0 and n % tn == 0 and tm % NORM_ROWS == 0
    out_shape = [jax.ShapeDtypeStruct((t, n), F32)]
    out_specs = [pl.BlockSpec((tm, tn), lambda i, j: (i, j))]
    scratch = [pltpu.VMEM((tm, d), BF16)]
    if emit_norm:
        assert tm % LANES == 0
        out_shape.append(jax.ShapeDtypeStruct((d, t), BF16))
        out_specs.append(pl.BlockSpec((d, tm), lambda i, j: (0, i)))
    res = pl.pallas_call(
        _norm_matmul_kernel,
        grid=(t // tm, n // tn),
        in_specs=[
            _row_tile_spec((tm, d), lambda i, j: (i, 0), single_buffer),
            pl.BlockSpec((1, d), lambda i, j: (0, 0)),
            pl.BlockSpec((d, tn), lambda i, j: (0, j)),
            pl.BlockSpec((1, tn), lambda i, j: (0, j)),
        ],
        out_specs=out_specs,
        out_shape=out_shape,
        scratch_shapes=scratch,
        compiler_params=_cparams(("parallel", "arbitrary")),
        name="norm_matmul_emit" if emit_norm else "norm_matmul",
    )(x, g, w, b)
    return res if emit_norm else res[0]


def _conv_kernel(val_ref, gate_ref, hist_ref, w_ref, bdw_ref, lng_ref, lnb_ref, gc_ref, *refs, tt, n_cast):
    u_ref, o_ref = refs[n_cast:n_cast + 2]
    ush_ref, par_ref = refs[2 * n_cast + 2:]
    _cast_slabs(refs[:n_cast], refs[n_cast + 2:2 * n_cast + 2])
    t = pl.program_id(1)
    n_rows = CONV_HALO + tt
    n_ch = w_ref.shape[1]

    @pl.when((pl.program_id(0) == 0) & (t == 0))
    def _():
        for k in range(CONV_WIDTH):
            par_ref[k] = jnp.broadcast_to(w_ref[k:k + 1, :], (SUBLANES, n_ch))
        for k, ref in enumerate((bdw_ref, lng_ref, lnb_ref, gc_ref)):
            par_ref[CONV_WIDTH + k] = jnp.broadcast_to(ref[...], (SUBLANES, n_ch))

    @pl.when(t == 0)
    def _():
        ush_ref[0, 0:CONV_HALO, :] = hist_ref[0]

    @pl.when(t > 0)
    def _():
        ush_ref[0, 0:CONV_HALO, :] = ush_ref[0, tt:tt + CONV_HALO, :]

    u = val_ref[...] * jax.nn.sigmoid(gate_ref[...])
    u_ref[...] = u
    ush_ref[0, CONV_HALO:n_rows, :] = u
    upad = ush_ref[0]
    for s in range(1, SUBLANES):
        ush_ref[s] = pltpu.roll(upad, n_rows - s, 0)

    rows = SUBLANES
    for c in range(tt // rows):
        r0 = c * rows
        acc = par_ref[CONV_WIDTH]
        for k in range(CONV_WIDTH):
            off = CONV_HALO - (CONV_WIDTH - 1) + k
            lo = r0 + off - off % SUBLANES
            acc = acc + ush_ref[off % SUBLANES, lo:lo + rows, :] * par_ref[k]
        mu = jnp.mean(acc, axis=-1, keepdims=True)
        cen = acc - mu
        var = jnp.mean(cen * cen, axis=-1, keepdims=True)
        yn = cen * lax.rsqrt(var + EPS) * par_ref[CONV_WIDTH + 1] + par_ref[CONV_WIDTH + 2]
        s = yn * jax.nn.sigmoid(yn)
        o_ref[r0:r0 + rows, :] = _rms(s, par_ref[CONV_WIDTH + 3]).astype(BF16)


def _conv_module(z, hist, w_dw, b_dw, ln_g, ln_b, g_conv, *, n_streams, tt, to_bf16=()):
    t_all = z.shape[0]
    c = w_dw.shape[1]
    t_stream = t_all // n_streams
    tt = min(tt, t_stream)
    nt = t_stream // tt
    slab_specs, slab_shapes = _slab_specs(to_bf16, n_streams * nt, lambda b, t: (b * nt + t, 0))
    res = pl.pallas_call(
        functools.partial(_conv_kernel, tt=tt, n_cast=len(to_bf16)),
        grid=(n_streams, nt),
        in_specs=[
            pl.BlockSpec((tt, c), lambda b, t: (b * nt + t, 0)),
            pl.BlockSpec((tt, c), lambda b, t: (b * nt + t, 1)),
            pl.BlockSpec((1, CONV_HALO, c), lambda b, t: (b, 0, 0)),
            pl.BlockSpec((CONV_WIDTH, c), lambda b, t: (0, 0)),
            pl.BlockSpec((1, c), lambda b, t: (0, 0)),
            pl.BlockSpec((1, c), lambda b, t: (0, 0)),
            pl.BlockSpec((1, c), lambda b, t: (0, 0)),
            pl.BlockSpec((1, c), lambda b, t: (0, 0)),
        ] + slab_specs,
        out_specs=[
            pl.BlockSpec((tt, c), lambda b, t: (b * nt + t, 0)),
            pl.BlockSpec((tt, c), lambda b, t: (b * nt + t, 0)),
        ] + slab_specs,
        out_shape=[jax.ShapeDtypeStruct((t_all, c), F32), jax.ShapeDtypeStruct((t_all, c), BF16)] + slab_shapes,
        scratch_shapes=[pltpu.VMEM((SUBLANES, CONV_HALO + tt, c), F32),
                        pltpu.VMEM((CONV_WIDTH + 4, SUBLANES, c), F32)],
        compiler_params=_cparams(("arbitrary", "arbitrary")),
        name="conv_module_cast" if to_bf16 else "conv_module",
    )(z, z, hist, w_dw, b_dw, ln_g, ln_b, g_conv, *to_bf16)
    return res[0], res[1], res[2:]


def _bias_row_index():
    m = np.arange(TOEPLITZ_N)
    dist = np.where(m <= TOEPLITZ_N - ATTN_Q_TILE, BAND_ROWS - m, BAND_ROWS + TOEPLITZ_N - m)
    return np.clip(dist, -REL_CLIP, REL_CLIP) + REL_CLIP


def _toeplitz(frow, rows):
    x = jnp.broadcast_to(frow, (rows, frow.shape[-1]))
    return pltpu.roll(x, 0, 1, stride=1, stride_axis=0)


def _attn_prompt_kernel(q_ref, k_ref, v_ref, f_ref, o_ref, kbuf_ref, vbuf_ref, bm_ref, *, t):
    i = pl.program_id(1)

    @pl.when(i == 0)
    def _():
        zeros = jnp.zeros((BAND_ROWS, HEAD_DIM), BF16)
        kbuf_ref[0:BAND_ROWS, :] = zeros
        vbuf_ref[0:BAND_ROWS, :] = zeros
        kbuf_ref[BAND_ROWS:BAND_ROWS + t, :] = k_ref[...].astype(BF16)
        vbuf_ref[BAND_ROWS:BAND_ROWS + t, :] = v_ref[...].astype(BF16)
        bias = _toeplitz(f_ref[0], ATTN_Q_TILE)[:, :ATTN_WINDOW]
        qc = lax.broadcasted_iota(jnp.int32, (ATTN_Q_TILE, ATTN_WINDOW), 0) // CHUNK
        kc = lax.broadcasted_iota(jnp.int32, (ATTN_Q_TILE, ATTN_WINDOW), 1) // CHUNK
        band = (kc >= qc) & (kc <= qc + N_PREV_CHUNKS)
        bm_ref[...] = jnp.where(band, bias, NEG_INF)

    col = lax.broadcasted_iota(jnp.int32, (ATTN_Q_TILE, ATTN_WINDOW), 1)
    for j in range(q_ref.shape[0] // ATTN_Q_TILE):
        rows = slice(j * ATTN_Q_TILE, (j + 1) * ATTN_Q_TILE)
        start = pl.multiple_of(i * q_ref.shape[0] + j * ATTN_Q_TILE, ATTN_Q_TILE)
        q = (q_ref[rows, :] * ATTN_SCALE).astype(BF16)
        kw = kbuf_ref[pl.ds(start, ATTN_WINDOW), :]
        vw = vbuf_ref[pl.ds(start, ATTN_WINDOW), :]
        s = lax.dot_general(q, kw, (((1,), (1,)), ((), ())), preferred_element_type=F32) + bm_ref[...]
        s = jnp.where(col >= BAND_ROWS - start, s, NEG_INF)
        m = jnp.max(s, axis=-1, keepdims=True)
        p = jnp.exp(s - m)
        l = jnp.sum(p, axis=-1, keepdims=True)
        o = jnp.dot(p.astype(BF16), vw, preferred_element_type=F32)
        o_ref[rows, :] = o / l


def _attn_prompt(z, frow, *, n_heads, q_blk, k_blk, v_blk, tq):
    t = z.shape[0]
    tq = min(tq, t)
    assert t % tq == 0 and tq % ATTN_Q_TILE == 0
    return pl.pallas_call(
        functools.partial(_attn_prompt_kernel, t=t),
        grid=(n_heads, t // tq),
        in_specs=[
            pl.BlockSpec((tq, HEAD_DIM), lambda h, i: (i, q_blk + h)),
            pl.BlockSpec((t, HEAD_DIM), lambda h, i: (0, k_blk + h)),
            pl.BlockSpec((t, HEAD_DIM), lambda h, i: (0, v_blk + h)),
            pl.BlockSpec((1, 1, TOEPLITZ_N), lambda h, i: (h, 0, 0)),
        ],
        out_specs=pl.BlockSpec((tq, HEAD_DIM), lambda h, i: (i, h)),
        out_shape=jax.ShapeDtypeStruct((t, n_heads * HEAD_DIM), F32),
        scratch_shapes=[
            pltpu.VMEM((BAND_ROWS + t, HEAD_DIM), BF16),
            pltpu.VMEM((BAND_ROWS + t, HEAD_DIM), BF16),
            pltpu.VMEM((ATTN_Q_TILE, ATTN_WINDOW), F32),
        ],
        compiler_params=_cparams(("arbitrary", "arbitrary")),
        name="attn_prompt",
    )(z, z, z, frow)


def _attn_sample_kernel(q_ref, k_ref, v_ref, ck_ref, cv_ref, f_ref, o_ref, bias_ref, *, n_heads, s_len, r_len):
    @pl.when(pl.program_id(0) == 0)
    def _():
        for h in range(n_heads):
            bias_ref[h] = _toeplitz(f_ref[h], s_len)

    for h in range(n_heads):
        hs = slice(h * HEAD_DIM, (h + 1) * HEAD_DIM)
        q = (q_ref[:, hs] * ATTN_SCALE).astype(BF16)
        kc = ck_ref[0, pl.ds(h, r_len, stride=n_heads), :].astype(BF16)
        vc = cv_ref[0, pl.ds(h, r_len, stride=n_heads), :].astype(BF16)
        kn = k_ref[:, hs].astype(BF16)
        vn = v_ref[:, hs].astype(BF16)
        nt = (((1,), (1,)), ((), ()))
        off = BAND_ROWS - r_len
        s_c = lax.dot_general(q, kc, nt, preferred_element_type=F32) + bias_ref[h, :, off:off + r_len]
        s_n = lax.dot_general(q, kn, nt, preferred_element_type=F32) + bias_ref[h, :, BAND_ROWS:BAND_ROWS + s_len]
        m = jnp.maximum(jnp.max(s_c, axis=-1, keepdims=True), jnp.max(s_n, axis=-1, keepdims=True))
        p_c = jnp.exp(s_c - m)
        p_n = jnp.exp(s_n - m)
        l = jnp.sum(p_c, axis=-1, keepdims=True) + jnp.sum(p_n, axis=-1, keepdims=True)
        o = (jnp.dot(p_c.astype(BF16), vc, preferred_element_type=F32)
             + jnp.dot(p_n.astype(BF16), vn, preferred_element_type=F32))
        o_ref[:, hs] = o / l


def _attn_sample(z, cache_k, cache_v, frow, *, n_streams, n_heads, q_blk, k_blk, v_blk):
    s_len = z.shape[0] // n_streams
    r_len = cache_k.shape[1] // n_heads
    a = n_heads * HEAD_DIM
    return pl.pallas_call(
        functools.partial(_attn_sample_kernel, n_heads=n_heads, s_len=s_len, r_len=r_len),
        grid=(n_streams,),
        in_specs=[
            pl.BlockSpec((s_len, a), lambda b: (b, q_blk)),
            pl.BlockSpec((s_len, a), lambda b: (b, k_blk)),
            pl.BlockSpec((s_len, a), lambda b: (b, v_blk)),
            pl.BlockSpec((1, r_len * n_heads, HEAD_DIM), lambda b: (b, 0, 0)),
            pl.BlockSpec((1, r_len * n_heads, HEAD_DIM), lambda b: (b, 0, 0)),
            pl.BlockSpec((n_heads, 1, TOEPLITZ_N), lambda b: (0, 0, 0)),
        ],
        out_specs=pl.BlockSpec((s_len, a), lambda b: (b, 0)),
        out_shape=jax.ShapeDtypeStruct((z.shape[0], a), F32),
        scratch_shapes=[pltpu.VMEM((n_heads, s_len, TOEPLITZ_N), F32)],
        compiler_params=_cparams(("arbitrary",)),
        name="attn_sample",
    )(z, z, z, cache_k, cache_v, frow)


def _merge_kernel(cn_ref, at_ref, ga_ref, w_ref, x_ref, o_ref, an_ref):
    c = cn_ref.shape[1]

    @pl.when(pl.program_id(1) == 0)
    def _():
        _norm_rows_to(an_ref, at_ref.shape[0], lambda rows: _rms(at_ref[rows, :], ga_ref[...]))

    o_ref[...] = (x_ref[...]
                  + jnp.dot(cn_ref[...], w_ref[0:c, :].astype(BF16), preferred_element_type=F32)
                  + jnp.dot(an_ref[...], w_ref[c:, :].astype(BF16), preferred_element_type=F32))


def _merge(conv_n, attn, g_attn, w_out, x, *, tm, tn):
    t, d = x.shape
    c = conv_n.shape[1]
    a = attn.shape[1]
    tm = min(tm, t)
    assert t % tm == 0 and d % tn == 0 and tm % NORM_ROWS == 0
    return pl.pallas_call(
        _merge_kernel,
        grid=(t // tm, d // tn),
        in_specs=[
            pl.BlockSpec((tm, c), lambda i, j: (i, 0)),
            pl.BlockSpec((tm, a), lambda i, j: (i, 0)),
            pl.BlockSpec((1, a), lambda i, j: (0, 0)),
            pl.BlockSpec((c + a, tn), lambda i, j: (0, j)),
            pl.BlockSpec((tm, tn), lambda i, j: (i, j)),
        ],
        out_specs=pl.BlockSpec((tm, tn), lambda i, j: (i, j)),
        out_shape=jax.ShapeDtypeStruct((t, d), F32),
        scratch_shapes=[pltpu.VMEM((tm, a), BF16)],
        compiler_params=_cparams(("parallel", "arbitrary")),
        name="merge",
    )(conv_n, attn, g_attn, w_out, x)


def _top16(scores, pos, n_pos):
    lanes = pos.shape[1]
    slot = lax.broadcasted_iota(jnp.int32, (PEER_TOPK, lanes), 0)

    def body(k, carry):
        out = []
        for s, vals, picks in carry:
            m = jnp.max(s, axis=0, keepdims=True)
            idx = jnp.min(jnp.where(s == m, pos, n_pos), axis=0, keepdims=True)
            s = jnp.where(pos == idx, -jnp.inf, s)
            vals = jnp.where(slot == k, m, vals)
            picks = jnp.where(slot == k, idx, picks)
            out.append((s, vals, picks))
        return tuple(out)

    init = tuple((s, jnp.zeros((PEER_TOPK, lanes), F32), jnp.zeros((PEER_TOPK, lanes), jnp.int32)) for s in scores)
    return lax.fori_loop(0, PEER_TOPK, body, init)


_GRID_HALF = PEER_TOPK // 2
SELECT_GROUP = 4


def _candidate_grid(v1, v2, iota8):
    neg = -jnp.inf
    rows = [v1[0:1, :] + v2]
    for a in range(1, _GRID_HALF):
        rows.append(jnp.where(iota8 < PEER_TOPK // (a + 1), v1[a:a + 1, :] + v2[0:_GRID_HALF, :], neg))
    rows.append(v1[_GRID_HALF:, :] + v2[0:1, :])
    return jnp.concatenate(rows, axis=0)


def _candidate_pos(iota8):
    pos = [iota8, iota8 + _GRID_HALF]
    pos += [iota8 + a * PEER_TOPK for a in range(1, _GRID_HALF)]
    pos.append((iota8 + _GRID_HALF) * PEER_TOPK)
    return jnp.concatenate(pos, axis=0)


def _kept_counts(left, iota8):
    gone = (left == -jnp.inf).astype(F32)
    cnt = [jnp.sum(gone[0:PEER_TOPK, :], axis=0, keepdims=True)]
    for a in range(1, _GRID_HALF):
        r0 = PEER_TOPK + (a - 1) * _GRID_HALF
        piece = jnp.where(iota8 < PEER_TOPK // (a + 1), gone[r0:r0 + _GRID_HALF, :], 0.0)
        cnt.append(jnp.sum(piece, axis=0, keepdims=True))
    r0 = PEER_TOPK + (_GRID_HALF - 1) * _GRID_HALF
    cnt += [gone[r0 + i:r0 + i + 1, :] for i in range(_GRID_HALF)]
    return cnt


def _peer_select_kernel(q_ref, keys_ref, *refs, group, n_cast):
    cnt_ref, e1_ref, r2_ref, e2_ref = refs[n_cast:n_cast + 4]
    _cast_slabs(refs[:n_cast], refs[n_cast + 4:])
    lanes = q_ref.shape[0]
    key_row = lax.broadcasted_iota(jnp.int32, (N_KEYS, lanes), 0)
    iota8 = lax.broadcasted_iota(jnp.int32, (_GRID_HALF, lanes), 0)
    grid_pos = _candidate_pos(iota8)
    nt = (((1,), (1,)), ((), ()))
    for h0 in range(0, PEER_HEADS, group):
        heads = range(h0, h0 + group)
        found = ()
        for h in heads:
            scores = []
            for p, e_ref in ((0, e1_ref), (1, e2_ref)):
                c0 = (h * 2 + p) * PEER_HALF
                qhp = q_ref[:, c0:c0 + PEER_HALF].astype(BF16)
                s = lax.dot_general(keys_ref[h, p], qhp, nt, preferred_element_type=F32)
                e_ref[h] = jnp.exp(s - jnp.max(s, axis=0, keepdims=True)).astype(e_ref.dtype)
                scores.append(s)
            found += tuple((vals, picks) for _, vals, picks in _top16(scores, key_row, N_KEYS))
        grids = []
        for i, h in enumerate(heads):
            (v1, _), (v2, picks2) = found[2 * i], found[2 * i + 1]
            r2 = jnp.full((N_KEYS, lanes), float(PEER_TOPK), F32)
            for k in range(PEER_TOPK):
                r2 = jnp.where(key_row == picks2[k:k + 1, :], float(k), r2)
            r2_ref[h] = r2.astype(r2_ref.dtype)
            grids.append(_candidate_grid(v1, v2, iota8))
        best = _top16(grids, grid_pos, PEER_TOPK * PEER_TOPK)
        for i, h in enumerate(heads):
            left, vals, _ = best[i]
            z = jnp.sum(jnp.exp(vals - vals[0:1, :]), axis=0, keepdims=True)
            cnt_a = _kept_counts(left, iota8)
            picks1 = found[2 * i][1]
            cnt = jnp.zeros((N_KEYS, lanes), F32)
            for k in range(PEER_TOPK):
                cnt = jnp.where(key_row == picks1[k:k + 1, :], cnt_a[k], cnt)
            cnt_ref[h] = cnt
            e1_ref[h] = e1_ref[h] * (1.0 / z)


def _peer_select(q, sub_keys, *, tq, to_bf16=()):
    t = q.shape[0]
    tq = min(tq, t)
    n_steps = t // tq
    tab = jax.ShapeDtypeStruct((PEER_HEADS, N_KEYS, t), F32)
    tab_bf = jax.ShapeDtypeStruct((PEER_HEADS, N_KEYS, t), BF16)
    tab_spec = pl.BlockSpec((PEER_HEADS, N_KEYS, tq), lambda i: (0, 0, i))
    slab_specs, slab_shapes = _slab_specs(to_bf16, n_steps, lambda i: (i, 0))
    res = pl.pallas_call(
        functools.partial(_peer_select_kernel, group=SELECT_GROUP, n_cast=len(to_bf16)),
        grid=(n_steps,),
        in_specs=[
            pl.BlockSpec((tq, q.shape[1]), lambda i: (i, 0)),
            pl.BlockSpec(sub_keys.shape, lambda i: (0, 0, 0, 0)),
        ] + slab_specs,
        out_specs=[tab_spec] * 4 + slab_specs,
        out_shape=[tab, tab, tab_bf, tab_bf] + slab_shapes,
        compiler_params=_cparams(("arbitrary",)),
        name="peer_select_cast" if to_bf16 else "peer_select",
    )(q, sub_keys, *to_bf16)
    return res[:4], res[4:]


def _peer_ffn_kernel(hnt_ref, u_ref, v_ref, cnt_ref, e1_ref, r2_ref, e2_ref, x1_ref, fg_ref, o_ref, act_ref, a_ref,
                     *, rows_per_tile, n_exp_tiles):
    s = pl.program_id(0)
    jp = jnp.maximum(s - 1, 0) % n_exp_tiles

    @pl.when(s == 0)
    def _():
        act_ref[...] = jnp.zeros_like(act_ref)

    @pl.when(jp == 0)
    def _():
        o_ref[...] = jnp.zeros_like(o_ref)

    tm = act_ref.shape[1]

    def finish_rows(r):
        row = (jp * rows_per_tile) % SUBLANES + r
        gate = jnp.zeros((N_KEYS, tm), BF16)
        for h in range(PEER_HEADS):
            cnt = cnt_ref[h, pl.ds(row, 1), :].astype(BF16)
            e1 = e1_ref[h, pl.ds(row, 1), :].astype(BF16)
            gate = gate + jnp.where(r2_ref[h] < cnt, e2_ref[h], jnp.zeros((), BF16)) * e1
        x = act_ref[r * N_KEYS:(r + 1) * N_KEYS, :]
        gelu = 0.5 * x * (1.0 + lax.erf(x * (2.0 ** -0.5)))
        a_ref[r * N_KEYS:(r + 1) * N_KEYS, :] = (gate.astype(F32) * gelu).astype(BF16)

    act_new = jnp.dot(u_ref[...], hnt_ref[...], preferred_element_type=F32)
    for r in range(rows_per_tile):
        finish_rows(r)
    tn = (((0,), (0,)), ((), ()))
    o_ref[...] += lax.dot_general(a_ref[...], v_ref[...], tn, preferred_element_type=F32)
    act_ref[...] = act_new

    @pl.when(jp == n_exp_tiles - 1)
    def _():
        def step(r, carry):
            rows = pl.ds(pl.multiple_of(r * NORM_ROWS, NORM_ROWS), NORM_ROWS)
            o_ref[rows, :] = _rms(x1_ref[rows, :] + o_ref[rows, :], fg_ref[...])
            return carry

        lax.fori_loop(0, tm // NORM_ROWS, step, 0)


def _peer_ffn(hn_t, u_bf, v_bf, tables, x1, final_g, *, tm, te):
    d, t = hn_t.shape
    n_exp = u_bf.shape[0]
    tm = min(tm, t)
    rows_per_tile = te // N_KEYS
    assert SUBLANES % rows_per_tile == 0 and t % tm == 0 and n_exp % te == 0
    ni, nj = t // tm, n_exp // te
    last = ni * nj - 1

    def cur(s):
        s = jnp.minimum(s, last)
        return s // nj, s % nj

    def prev(s):
        s = jnp.maximum(s - 1, 0)
        return s // nj, s % nj

    tab_spec = pl.BlockSpec((PEER_HEADS, N_KEYS, tm), lambda s: (0, 0, prev(s)[0]))
    row_spec = pl.BlockSpec((PEER_HEADS, SUBLANES, tm),
                            lambda s: (0, (prev(s)[1] * rows_per_tile) // SUBLANES, prev(s)[0]))
    return pl.pallas_call(
        functools.partial(_peer_ffn_kernel, rows_per_tile=rows_per_tile, n_exp_tiles=nj),
        grid=(ni * nj + 1,),
        in_specs=[
            pl.BlockSpec((d, tm), lambda s: (0, cur(s)[0])),
            pl.BlockSpec((te, d), lambda s: (cur(s)[1], 0)),
            pl.BlockSpec((te, d), lambda s: (prev(s)[1], 0)),
            row_spec, row_spec, tab_spec, tab_spec,
            _row_tile_spec((tm, d), lambda s: (prev(s)[0], 0), True),
            pl.BlockSpec((1, d), lambda s: (0, 0)),
        ],
        out_specs=pl.BlockSpec((tm, d), lambda s: (prev(s)[0], 0)),
        out_shape=jax.ShapeDtypeStruct((t, d), F32),
        scratch_shapes=[pltpu.VMEM((te, tm), F32), pltpu.VMEM((te, tm), BF16)],
        compiler_params=_cparams(("arbitrary",)),
        name="peer_ffn",
    )(hn_t, u_bf, v_bf, *tables, x1, final_g)


def _group(x, hist, cache_kv, p, *, n_streams):
    c = p["w_dw"].shape[1]
    a = p["g_attn"].shape[1]
    n_heads = a // HEAD_DIM
    tl = TILES
    z = _norm_matmul(x, p["mix_g"], p["w_in"], p["b_in"], tm=tl["proj_rows"], tn=tl["proj_cols"],
                     single_buffer=True)
    pending = (p["w_out"], p["w_q"]) if p["w_out"].dtype != BF16 else ()
    u, conv_n, done = _conv_module(z, hist, p["w_dw"], p["b_dw"], p["ln_g"], p["ln_b"], p["g_conv"],
                                   n_streams=n_streams, tt=tl["conv_rows"], to_bf16=pending)
    if done:
        p = dict(p, w_out=done[0], w_q=done[1])
    if cache_kv is None:
        blk = 2 * c // HEAD_DIM
        attn = _attn_prompt(z, p["frow"], n_heads=n_heads, q_blk=blk, k_blk=blk + n_heads, v_blk=blk + 2 * n_heads,
                            tq=tl["attn_rows"])
    else:
        blk = 2 * c // a
        attn = _attn_sample(z, cache_kv[0], cache_kv[1], p["frow"], n_streams=n_streams, n_heads=n_heads,
                            q_blk=blk, k_blk=blk + 1, v_blk=blk + 2)
    x1 = _merge(conv_n, attn, p["g_attn"], p["w_out"], x, tm=tl["proj_rows"], tn=tl["proj_cols"])
    q, hn_t = _norm_matmul(x1, p["ffn_g"], p["w_q"], p["zero_bq"], tm=tl["proj_rows"], tn=tl["proj_cols"],
                         emit_norm=True, single_buffer=True)
    pending = (p["peer_u"], p["peer_v"]) if p["peer_u"].dtype != BF16 else ()
    tables, done = _peer_select(q, p["sub_keys"], tq=tl["select_tokens"], to_bf16=pending)
    if done:
        p = dict(p, peer_u=done[0], peer_v=done[1])
    y = _peer_ffn(hn_t, p["peer_u"], p["peer_v"], tables, x1, p["final_g"], tm=tl["ffn_tokens"], te=tl["ffn_experts"])
    return y, z, u, p


def kernel(x_prompt, x_sample, cache_conv, cache_k, cache_v, mix_norm_g, w_in, b_in, w_dw, b_dw, ln_g, ln_b,
           rel_bias, out_norm_conv_g, out_norm_attn_g, w_out, ffn_norm_g, peer_w_q, peer_sub_keys, peer_u, peer_v,
           final_norm_g):
    assert mix_norm_g.shape[0] == 1, "single-layer step"
    bsz, seq, d = x_prompt.shape
    db, ds, _ = x_sample.shape
    c = w_dw.shape[2]
    a = out_norm_attn_g.shape[1]
    n_heads = a // HEAD_DIM
    assert bsz == 1 and seq % ATTN_Q_TILE == 0 and ds >= CONV_WIDTH - 1

    row = lambda v: v.reshape(1, -1)
    p = dict(
        mix_g=row(mix_norm_g[0]), w_in=w_in[0], b_in=row(b_in[0]),
        w_dw=w_dw[0], b_dw=row(b_dw[0]), ln_g=row(ln_g[0]), ln_b=row(ln_b[0]),
        g_conv=row(out_norm_conv_g[0]), g_attn=row(out_norm_attn_g[0]), w_out=w_out[0],
        ffn_g=row(ffn_norm_g[0]), w_q=peer_w_q[0],
        zero_bq=jnp.zeros((1, peer_w_q.shape[2]), F32),
        sub_keys=peer_sub_keys[0].astype(BF16), peer_u=peer_u[0], peer_v=peer_v[0],
        final_g=row(final_norm_g),
        frow=rel_bias[0][:, _bias_row_index()].reshape(n_heads, 1, TOEPLITZ_N),
    )

    hist_p = jnp.zeros((1, CONV_HALO, c), F32)
    y_p, z_p, u_p, p = _group(x_prompt.reshape(seq, d), hist_p, None, p, n_streams=1)

    pad = CONV_HALO - (CONV_WIDTH - 1)
    hist_s = jnp.pad(cache_conv[0], ((0, 0), (pad, 0), (0, 0)))
    r_len = cache_k.shape[2]
    cache_kv = (cache_k[0].reshape(db, r_len * n_heads, HEAD_DIM), cache_v[0].reshape(db, r_len * n_heads, HEAD_DIM))
    y_s, z_s, u_s, _ = _group(x_sample.reshape(db * ds, d), hist_s, cache_kv, p, n_streams=db)

    k0, v0 = 2 * c + a, 2 * c + 2 * a
    rows_p = min(BAND_ROWS, seq)
    keep = CONV_WIDTH - 1
    return (
        y_p.reshape(1, seq, d),
        y_s.reshape(db, ds, d),
        u_p[seq - keep:].reshape(1, 1, keep, c),
        z_p[seq - rows_p:, k0:k0 + a].reshape(1, 1, rows_p, n_heads, HEAD_DIM),
        z_p[seq - rows_p:, v0:v0 + a].reshape(1, 1, rows_p, n_heads, HEAD_DIM),
        u_s.reshape(db, ds, c)[:, ds - keep:].reshape(1, db, keep, c),
        z_s[:, k0:k0 + a].reshape(1, db, ds, n_heads, HEAD_DIM),
        z_s[:, v0:v0 + a].reshape(1, db, ds, n_heads, HEAD_DIM),
    )
```

```python
import functools

import numpy as np
import jax
import jax.numpy as jnp
from jax import lax
from jax.experimental import pallas as pl
from jax.experimental.pallas import tpu as pltpu

F32 = jnp.float32
BF16 = jnp.bfloat16

CHUNK = 64
CONV_WIDTH = 31
CONV_HALO = 32
HEAD_DIM = 128
N_PREV_CHUNKS = 8
BAND_ROWS = N_PREV_CHUNKS * CHUNK
REL_CLIP = 256
ATTN_SCALE = HEAD_DIM ** -0.5
PEER_HEADS = 8
PEER_HALF = 128
N_KEYS = 128
PEER_TOPK = 16
SUBLANES = 8
LANES = 128
MXU_WIDTH = 256
EPS = 1e-6
NEG_INF = -1e30

ATTN_Q_TILE = 4 * CHUNK
ATTN_WINDOW = BAND_ROWS + ATTN_Q_TILE
TOEPLITZ_N = 1024

VMEM_LIMIT = 56 * 1024 * 1024

TILES = dict(
    proj_rows=1024, proj_cols=512,
    conv_rows=128,
    attn_rows=4096,
    select_tokens=128,
    ffn_tokens=512, ffn_experts=512,
)


def _cparams(sem, flags=None):
    return pltpu.CompilerParams(dimension_semantics=sem, vmem_limit_bytes=VMEM_LIMIT, flags=flags)


def _rms(x, g):
    ms = jnp.mean(x * x, axis=-1, keepdims=True)
    return x * lax.rsqrt(ms + EPS) * g


def _slab_specs(mats, n_steps, index_map):
    specs, shapes = [], []
    for m in mats:
        rows = m.shape[0] // n_steps
        assert m.shape[0] % n_steps == 0 and rows % 16 == 0, (m.shape, n_steps)
        specs.append(pl.BlockSpec((rows, m.shape[1]), index_map))
        shapes.append(jax.ShapeDtypeStruct(m.shape, BF16))
    return specs, shapes


def _cast_slabs(src_refs, dst_refs):
    for src, dst in zip(src_refs, dst_refs, strict=True):
        dst[...] = src[...].astype(BF16)


NORM_ROWS = 64


def _norm_rows_to(dst_ref, n_rows, row_fn):
    def step(r, carry):
        rows = pl.ds(pl.multiple_of(r * NORM_ROWS, NORM_ROWS), NORM_ROWS)
        dst_ref[rows, :] = row_fn(rows).astype(BF16)
        return carry

    lax.fori_loop(0, n_rows // NORM_ROWS, step, 0)


def _norm_matmul_kernel(x_ref, g_ref, w_ref, b_ref, o_ref, *rest):
    xn_ref = rest[-1]

    @pl.when(pl.program_id(1) == 0)
    def _():
        _norm_rows_to(xn_ref, x_ref.shape[0], lambda rows: _rms(x_ref[rows, :], g_ref[...]))
        if len(rest) == 2:
            def step(r, carry):
                rows = pl.ds(pl.multiple_of(r * LANES, LANES), LANES)
                rest[0][:, rows] = xn_ref[rows, :].astype(F32).T.astype(BF16)
                return carry

            lax.fori_loop(0, x_ref.shape[0] // LANES, step, 0)

    o_ref[...] = jnp.dot(xn_ref[...], w_ref[...].astype(BF16), preferred_element_type=F32) + b_ref[...]


def _row_tile_spec(block_shape, index_map, single_buffer):
    if single_buffer:
        return pl.BlockSpec(block_shape, index_map, pipeline_mode=pl.Buffered(1))
    return pl.BlockSpec(block_shape, index_map)


def _norm_matmul(x, g, w, b, *, tm, tn, emit_norm=False, single_buffer=False):
    t, d = x.shape
    n = w.shape[1]
    tm = min(tm, t)
    assert t % tm == 0 and n % tn == 0 and tm % NORM_ROWS == 0
    out_shape = [jax.ShapeDtypeStruct((t, n), F32)]
    out_specs = [pl.BlockSpec((tm, tn), lambda i, j: (i, j))]
    scratch = [pltpu.VMEM((tm, d), BF16)]
    if emit_norm:
        assert tm % LANES == 0
        out_shape.append(jax.ShapeDtypeStruct((d, t), BF16))
        out_specs.append(pl.BlockSpec((d, tm), lambda i, j: (0, i)))
    res = pl.pallas_call(
        _norm_matmul_kernel,
        grid=(t // tm, n // tn),
        in_specs=[
            _row_tile_spec((tm, d), lambda i, j: (i, 0), single_buffer),
            pl.BlockSpec((1, d), lambda i, j: (0, 0)),
            pl.BlockSpec((d, tn), lambda i, j: (0, j)),
            pl.BlockSpec((1, tn), lambda i, j: (0, j)),
        ],
        out_specs=out_specs,
        out_shape=out_shape,
        scratch_shapes=scratch,
        compiler_params=_cparams(("parallel", "arbitrary")),
        name="norm_matmul_emit" if emit_norm else "norm_matmul",
    )(x, g, w, b)
    return res if emit_norm else res[0]


def _conv_kernel(val_ref, gate_ref, hist_ref, w_ref, bdw_ref, lng_ref, lnb_ref, gc_ref, *refs, tt, n_cast):
    u_ref, o_ref = refs[n_cast:n_cast + 2]
    ush_ref, par_ref = refs[2 * n_cast + 2:]
    _cast_slabs(refs[:n_cast], refs[n_cast + 2:2 * n_cast + 2])
    t = pl.program_id(1)
    n_rows = CONV_HALO + tt
    n_ch = w_ref.shape[1]

    @pl.when((pl.program_id(0) == 0) & (t == 0))
    def _():
        for k in range(CONV_WIDTH):
            par_ref[k] = jnp.broadcast_to(w_ref[k:k + 1, :], (SUBLANES, n_ch))
        for k, ref in enumerate((bdw_ref, lng_ref, lnb_ref, gc_ref)):
            par_ref[CONV_WIDTH + k] = jnp.broadcast_to(ref[...], (SUBLANES, n_ch))

    @pl.when(t == 0)
    def _():
        ush_ref[0, 0:CONV_HALO, :] = hist_ref[0]

    @pl.when(t > 0)
    def _():
        ush_ref[0, 0:CONV_HALO, :] = ush_ref[0, tt:tt + CONV_HALO, :]

    u = val_ref[...] * jax.nn.sigmoid(gate_ref[...])
    u_ref[...] = u
    ush_ref[0, CONV_HALO:n_rows, :] = u
    upad = ush_ref[0]
    for s in range(1, SUBLANES):
        ush_ref[s] = pltpu.roll(upad, n_rows - s, 0)

    rows = SUBLANES
    for c in range(tt // rows):
        r0 = c * rows
        acc = par_ref[CONV_WIDTH]
        for k in range(CONV_WIDTH):
            off = CONV_HALO - (CONV_WIDTH - 1) + k
            lo = r0 + off - off % SUBLANES
            acc = acc + ush_ref[off % SUBLANES, lo:lo + rows, :] * par_ref[k]
        mu = jnp.mean(acc, axis=-1, keepdims=True)
        cen = acc - mu
        var = jnp.mean(cen * cen, axis=-1, keepdims=True)
        yn = cen * lax.rsqrt(var + EPS) * par_ref[CONV_WIDTH + 1] + par_ref[CONV_WIDTH + 2]
        s = yn * jax.nn.sigmoid(yn)
        o_ref[r0:r0 + rows, :] = _rms(s, par_ref[CONV_WIDTH + 3]).astype(BF16)


def _conv_module(z, hist, w_dw, b_dw, ln_g, ln_b, g_conv, *, n_streams, tt, to_bf16=()):
    t_all = z.shape[0]
    c = w_dw.shape[1]
    t_stream = t_all // n_streams
    tt = min(tt, t_stream)
    nt = t_stream // tt
    slab_specs, slab_shapes = _slab_specs(to_bf16, n_streams * nt, lambda b, t: (b * nt + t, 0))
    res = pl.pallas_call(
        functools.partial(_conv_kernel, tt=tt, n_cast=len(to_bf16)),
        grid=(n_streams, nt),
        in_specs=[
            pl.BlockSpec((tt, c), lambda b, t: (b * nt + t, 0)),
            pl.BlockSpec((tt, c), lambda b, t: (b * nt + t, 1)),
            pl.BlockSpec((1, CONV_HALO, c), lambda b, t: (b, 0, 0)),
            pl.BlockSpec((CONV_WIDTH, c), lambda b, t: (0, 0)),
            pl.BlockSpec((1, c), lambda b, t: (0, 0)),
            pl.BlockSpec((1, c), lambda b, t: (0, 0)),
            pl.BlockSpec((1, c), lambda b, t: (0, 0)),
            pl.BlockSpec((1, c), lambda b, t: (0, 0)),
        ] + slab_specs,
        out_specs=[
            pl.BlockSpec((tt, c), lambda b, t: (b * nt + t, 0)),
            pl.BlockSpec((tt, c), lambda b, t: (b * nt + t, 0)),
        ] + slab_specs,
        out_shape=[jax.ShapeDtypeStruct((t_all, c), F32), jax.ShapeDtypeStruct((t_all, c), BF16)] + slab_shapes,
        scratch_shapes=[pltpu.VMEM((SUBLANES, CONV_HALO + tt, c), F32),
                        pltpu.VMEM((CONV_WIDTH + 4, SUBLANES, c), F32)],
        compiler_params=_cparams(("arbitrary", "arbitrary")),
        name="conv_module_cast" if to_bf16 else "conv_module",
    )(z, z, hist, w_dw, b_dw, ln_g, ln_b, g_conv, *to_bf16)
    return res[0], res[1], res[2:]


def _bias_row_index():
    m = np.arange(TOEPLITZ_N)
    dist = np.where(m <= TOEPLITZ_N - ATTN_Q_TILE, BAND_ROWS - m, BAND_ROWS + TOEPLITZ_N - m)
    return np.clip(dist, -REL_CLIP, REL_CLIP) + REL_CLIP


def _toeplitz(frow, rows):
    x = jnp.broadcast_to(frow, (rows, frow.shape[-1]))
    return pltpu.roll(x, 0, 1, stride=1, stride_axis=0)


def _attn_prompt_kernel(q_ref, k_ref, v_ref, f_ref, o_ref, kbuf_ref, vbuf_ref, bm_ref, *, t):
    i = pl.program_id(1)

    @pl.when(i == 0)
    def _():
        zeros = jnp.zeros((BAND_ROWS, HEAD_DIM), BF16)
        kbuf_ref[0:BAND_ROWS, :] = zeros
        vbuf_ref[0:BAND_ROWS, :] = zeros
        kbuf_ref[BAND_ROWS:BAND_ROWS + t, :] = k_ref[...].astype(BF16)
        vbuf_ref[BAND_ROWS:BAND_ROWS + t, :] = v_ref[...].astype(BF16)
        bias = _toeplitz(f_ref[0], ATTN_Q_TILE)[:, :ATTN_WINDOW]
        qc = lax.broadcasted_iota(jnp.int32, (ATTN_Q_TILE, ATTN_WINDOW), 0) // CHUNK
        kc = lax.broadcasted_iota(jnp.int32, (ATTN_Q_TILE, ATTN_WINDOW), 1) // CHUNK
        band = (kc >= qc) & (kc <= qc + N_PREV_CHUNKS)
        bm_ref[...] = jnp.where(band, bias, NEG_INF)

    col = lax.broadcasted_iota(jnp.int32, (ATTN_Q_TILE, ATTN_WINDOW), 1)
    for j in range(q_ref.shape[0] // ATTN_Q_TILE):
        rows = slice(j * ATTN_Q_TILE, (j + 1) * ATTN_Q_TILE)
        start = pl.multiple_of(i * q_ref.shape[0] + j * ATTN_Q_TILE, ATTN_Q_TILE)
        q = (q_ref[rows, :] * ATTN_SCALE).astype(BF16)
        kw = kbuf_ref[pl.ds(start, ATTN_WINDOW), :]
        vw = vbuf_ref[pl.ds(start, ATTN_WINDOW), :]
        s = lax.dot_general(q, kw, (((1,), (1,)), ((), ())), preferred_element_type=F32) + bm_ref[...]
        s = jnp.where(col >= BAND_ROWS - start, s, NEG_INF)
        m = jnp.max(s, axis=-1, keepdims=True)
        p = jnp.exp(s - m)
        l = jnp.sum(p, axis=-1, keepdims=True)
        o = jnp.dot(p.astype(BF16), vw, preferred_element_type=F32)
        o_ref[rows, :] = o / l


def _attn_prompt(z, frow, *, n_heads, q_blk, k_blk, v_blk, tq):
    t = z.shape[0]
    tq = min(tq, t)
    assert t % tq == 0 and tq % ATTN_Q_TILE == 0
    return pl.pallas_call(
        functools.partial(_attn_prompt_kernel, t=t),
        grid=(n_heads, t // tq),
        in_specs=[
            pl.BlockSpec((tq, HEAD_DIM), lambda h, i: (i, q_blk + h)),
            pl.BlockSpec((t, HEAD_DIM), lambda h, i: (0, k_blk + h)),
            pl.BlockSpec((t, HEAD_DIM), lambda h, i: (0, v_blk + h)),
            pl.BlockSpec((1, 1, TOEPLITZ_N), lambda h, i: (h, 0, 0)),
        ],
        out_specs=pl.BlockSpec((tq, HEAD_DIM), lambda h, i: (i, h)),
        out_shape=jax.ShapeDtypeStruct((t, n_heads * HEAD_DIM), F32),
        scratch_shapes=[
            pltpu.VMEM((BAND_ROWS + t, HEAD_DIM), BF16),
            pltpu.VMEM((BAND_ROWS + t, HEAD_DIM), BF16),
            pltpu.VMEM((ATTN_Q_TILE, ATTN_WINDOW), F32),
        ],
        compiler_params=_cparams(("arbitrary", "arbitrary")),
        name="attn_prompt",
    )(z, z, z, frow)


def _attn_sample_kernel(q_ref, k_ref, v_ref, ck_ref, cv_ref, f_ref, o_ref, bias_ref, *, n_heads, s_len, r_len):
    @pl.when(pl.program_id(0) == 0)
    def _():
        for h in range(n_heads):
            bias_ref[h] = _toeplitz(f_ref[h], s_len)

    for h in range(n_heads):
        hs = slice(h * HEAD_DIM, (h + 1) * HEAD_DIM)
        q = (q_ref[:, hs] * ATTN_SCALE).astype(BF16)
        kc = ck_ref[0, pl.ds(h, r_len, stride=n_heads), :].astype(BF16)
        vc = cv_ref[0, pl.ds(h, r_len, stride=n_heads), :].astype(BF16)
        kn = k_ref[:, hs].astype(BF16)
        vn = v_ref[:, hs].astype(BF16)
        nt = (((1,), (1,)), ((), ()))
        off = BAND_ROWS - r_len
        s_c = lax.dot_general(q, kc, nt, preferred_element_type=F32) + bias_ref[h, :, off:off + r_len]
        s_n = lax.dot_general(q, kn, nt, preferred_element_type=F32) + bias_ref[h, :, BAND_ROWS:BAND_ROWS + s_len]
        m = jnp.maximum(jnp.max(s_c, axis=-1, keepdims=True), jnp.max(s_n, axis=-1, keepdims=True))
        p_c = jnp.exp(s_c - m)
        p_n = jnp.exp(s_n - m)
        l = jnp.sum(p_c, axis=-1, keepdims=True) + jnp.sum(p_n, axis=-1, keepdims=True)
        o = (jnp.dot(p_c.astype(BF16), vc, preferred_element_type=F32)
             + jnp.dot(p_n.astype(BF16), vn, preferred_element_type=F32))
        o_ref[:, hs] = o / l


def _attn_sample(z, cache_k, cache_v, frow, *, n_streams, n_heads, q_blk, k_blk, v_blk):
    s_len = z.shape[0] // n_streams
    r_len = cache_k.shape[1] // n_heads
    a = n_heads * HEAD_DIM
    return pl.pallas_call(
        functools.partial(_attn_sample_kernel, n_heads=n_heads, s_len=s_len, r_len=r_len),
        grid=(n_streams,),
        in_specs=[
            pl.BlockSpec((s_len, a), lambda b: (b, q_blk)),
            pl.BlockSpec((s_len, a), lambda b: (b, k_blk)),
            pl.BlockSpec((s_len, a), lambda b: (b, v_blk)),
            pl.BlockSpec((1, r_len * n_heads, HEAD_DIM), lambda b: (b, 0, 0)),
            pl.BlockSpec((1, r_len * n_heads, HEAD_DIM), lambda b: (b, 0, 0)),
            pl.BlockSpec((n_heads, 1, TOEPLITZ_N), lambda b: (0, 0, 0)),
        ],
        out_specs=pl.BlockSpec((s_len, a), lambda b: (b, 0)),
        out_shape=jax.ShapeDtypeStruct((z.shape[0], a), F32),
        scratch_shapes=[pltpu.VMEM((n_heads, s_len, TOEPLITZ_N), F32)],
        compiler_params=_cparams(("arbitrary",)),
        name="attn_sample",
    )(z, z, z, cache_k, cache_v, frow)


def _merge_kernel(cn_ref, at_ref, ga_ref, w_ref, x_ref, o_ref, an_ref):
    c = cn_ref.shape[1]

    @pl.when(pl.program_id(1) == 0)
    def _():
        _norm_rows_to(an_ref, at_ref.shape[0], lambda rows: _rms(at_ref[rows, :], ga_ref[...]))

    o_ref[...] = (x_ref[...]
                  + jnp.dot(cn_ref[...], w_ref[0:c, :].astype(BF16), preferred_element_type=F32)
                  + jnp.dot(an_ref[...], w_ref[c:, :].astype(BF16), preferred_element_type=F32))


def _merge(conv_n, attn, g_attn, w_out, x, *, tm, tn):
    t, d = x.shape
    c = conv_n.shape[1]
    a = attn.shape[1]
    tm = min(tm, t)
    assert t % tm == 0 and d % tn == 0 and tm % NORM_ROWS == 0
    return pl.pallas_call(
        _merge_kernel,
        grid=(t // tm, d // tn),
        in_specs=[
            pl.BlockSpec((tm, c), lambda i, j: (i, 0)),
            pl.BlockSpec((tm, a), lambda i, j: (i, 0)),
            pl.BlockSpec((1, a), lambda i, j: (0, 0)),
            pl.BlockSpec((c + a, tn), lambda i, j: (0, j)),
            pl.BlockSpec((tm, tn), lambda i, j: (i, j)),
        ],
        out_specs=pl.BlockSpec((tm, tn), lambda i, j: (i, j)),
        out_shape=jax.ShapeDtypeStruct((t, d), F32),
        scratch_shapes=[pltpu.VMEM((tm, a), BF16)],
        compiler_params=_cparams(("parallel", "arbitrary")),
        name="merge",
    )(conv_n, attn, g_attn, w_out, x)


def _top16(scores, pos, n_pos):
    lanes = pos.shape[1]
    slot = lax.broadcasted_iota(jnp.int32, (PEER_TOPK, lanes), 0)

    def body(k, carry):
        out = []
        for s, vals, picks in carry:
            m = jnp.max(s, axis=0, keepdims=True)
            idx = jnp.min(jnp.where(s == m, pos, n_pos), axis=0, keepdims=True)
            s = jnp.where(pos == idx, -jnp.inf, s)
            vals = jnp.where(slot == k, m, vals)
            picks = jnp.where(slot == k, idx, picks)
            out.append((s, vals, picks))
        return tuple(out)

    init = tuple((s, jnp.zeros((PEER_TOPK, lanes), F32), jnp.zeros((PEER_TOPK, lanes), jnp.int32)) for s in scores)
    return lax.fori_loop(0, PEER_TOPK, body, init)


_GRID_HALF = PEER_TOPK // 2
SELECT_GROUP = 4


def _candidate_grid(v1, v2, iota8):
    neg = -jnp.inf
    rows = [v1[0:1, :] + v2]
    for a in range(1, _GRID_HALF):
        rows.append(jnp.where(iota8 < PEER_TOPK // (a + 1), v1[a:a + 1, :] + v2[0:_GRID_HALF, :], neg))
    rows.append(v1[_GRID_HALF:, :] + v2[0:1, :])
    return jnp.concatenate(rows, axis=0)


def _candidate_pos(iota8):
    pos = [iota8, iota8 + _GRID_HALF]
    pos += [iota8 + a * PEER_TOPK for a in range(1, _GRID_HALF)]
    pos.append((iota8 + _GRID_HALF) * PEER_TOPK)
    return jnp.concatenate(pos, axis=0)


def _kept_counts(left, iota8):
    gone = (left == -jnp.inf).astype(F32)
    cnt = [jnp.sum(gone[0:PEER_TOPK, :], axis=0, keepdims=True)]
    for a in range(1, _GRID_HALF):
        r0 = PEER_TOPK + (a - 1) * _GRID_HALF
        piece = jnp.where(iota8 < PEER_TOPK // (a + 1), gone[r0:r0 + _GRID_HALF, :], 0.0)
        cnt.append(jnp.sum(piece, axis=0, keepdims=True))
    r0 = PEER_TOPK + (_GRID_HALF - 1) * _GRID_HALF
    cnt += [gone[r0 + i:r0 + i + 1, :] for i in range(_GRID_HALF)]
    return cnt


def _peer_select_kernel(q_ref, keys_ref, *refs, group, n_cast):
    cnt_ref, e1_ref, r2_ref, e2_ref = refs[n_cast:n_cast + 4]
    _cast_slabs(refs[:n_cast], refs[n_cast + 4:])
    lanes = q_ref.shape[0]
    key_row = lax.broadcasted_iota(jnp.int32, (N_KEYS, lanes), 0)
    iota8 = lax.broadcasted_iota(jnp.int32, (_GRID_HALF, lanes), 0)
    grid_pos = _candidate_pos(iota8)
    nt = (((1,), (1,)), ((), ()))
    for h0 in range(0, PEER_HEADS, group):
        heads = range(h0, h0 + group)
        found = ()
        for h in heads:
            scores = []
            for p, e_ref in ((0, e1_ref), (1, e2_ref)):
                c0 = (h * 2 + p) * PEER_HALF
                qhp = q_ref[:, c0:c0 + PEER_HALF].astype(BF16)
                s = lax.dot_general(keys_ref[h, p], qhp, nt, preferred_element_type=F32)
                e_ref[h] = jnp.exp(s - jnp.max(s, axis=0, keepdims=True)).astype(e_ref.dtype)
                scores.append(s)
            found += tuple((vals, picks) for _, vals, picks in _top16(scores, key_row, N_KEYS))
        grids = []
        for i, h in enumerate(heads):
            (v1, _), (v2, picks2) = found[2 * i], found[2 * i + 1]
            r2 = jnp.full((N_KEYS, lanes), float(PEER_TOPK), F32)
            for k in range(PEER_TOPK):
                r2 = jnp.where(key_row == picks2[k:k + 1, :], float(k), r2)
            r2_ref[h] = r2.astype(r2_ref.dtype)
            grids.append(_candidate_grid(v1, v2, iota8))
        best = _top16(grids, grid_pos, PEER_TOPK * PEER_TOPK)
        for i, h in enumerate(heads):
            left, vals, _ = best[i]
            z = jnp.sum(jnp.exp(vals - vals[0:1, :]), axis=0, keepdims=True)
            cnt_a = _kept_counts(left, iota8)
            picks1 = found[2 * i][1]
            cnt = jnp.zeros((N_KEYS, lanes), F32)
            for k in range(PEER_TOPK):
                cnt = jnp.where(key_row == picks1[k:k + 1, :], cnt_a[k], cnt)
            cnt_ref[h] = cnt
            e1_ref[h] = e1_ref[h] * (1.0 / z)


def _peer_select(q, sub_keys, *, tq, to_bf16=()):
    t = q.shape[0]
    tq = min(tq, t)
    n_steps = t // tq
    tab = jax.ShapeDtypeStruct((PEER_HEADS, N_KEYS, t), F32)
    tab_bf = jax.ShapeDtypeStruct((PEER_HEADS, N_KEYS, t), BF16)
    tab_spec = pl.BlockSpec((PEER_HEADS, N_KEYS, tq), lambda i: (0, 0, i))
    slab_specs, slab_shapes = _slab_specs(to_bf16, n_steps, lambda i: (i, 0))
    res = pl.pallas_call(
        functools.partial(_peer_select_kernel, group=SELECT_GROUP, n_cast=len(to_bf16)),
        grid=(n_steps,),
        in_specs=[
            pl.BlockSpec((tq, q.shape[1]), lambda i: (i, 0)),
            pl.BlockSpec(sub_keys.shape, lambda i: (0, 0, 0, 0)),
        ] + slab_specs,
        out_specs=[tab_spec] * 4 + slab_specs,
        out_shape=[tab, tab, tab_bf, tab_bf] + slab_shapes,
        compiler_params=_cparams(("arbitrary",)),
        name="peer_select_cast" if to_bf16 else "peer_select",
    )(q, sub_keys, *to_bf16)
    return res[:4], res[4:]


def _peer_ffn_kernel(hnt_ref, u_ref, v_ref, cnt_ref, e1_ref, r2_ref, e2_ref, x1_ref, fg_ref, o_ref, act_ref, a_ref,
                     *, rows_per_tile, n_exp_tiles):
    s = pl.program_id(0)
    jp = jnp.maximum(s - 1, 0) % n_exp_tiles

    @pl.when(s == 0)
    def _():
        act_ref[...] = jnp.zeros_like(act_ref)

    @pl.when(jp == 0)
    def _():
        o_ref[...] = jnp.zeros_like(o_ref)

    tm = act_ref.shape[1]

    def finish_rows(r):
        row = (jp * rows_per_tile) % SUBLANES + r
        gate = jnp.zeros((N_KEYS, tm), BF16)
        for h in range(PEER_HEADS):
            cnt = cnt_ref[h, pl.ds(row, 1), :].astype(BF16)
            e1 = e1_ref[h, pl.ds(row, 1), :].astype(BF16)
            gate = gate + jnp.where(r2_ref[h] < cnt, e2_ref[h], jnp.zeros((), BF16)) * e1
        x = act_ref[r * N_KEYS:(r + 1) * N_KEYS, :]
        gelu = 0.5 * x * (1.0 + lax.erf(x * (2.0 ** -0.5)))
        a_ref[r * N_KEYS:(r + 1) * N_KEYS, :] = (gate.astype(F32) * gelu).astype(BF16)

    act_new = jnp.dot(u_ref[...], hnt_ref[...], preferred_element_type=F32)
    for r in range(rows_per_tile):
        finish_rows(r)
    tn = (((0,), (0,)), ((), ()))
    o_ref[...] += lax.dot_general(a_ref[...], v_ref[...], tn, preferred_element_type=F32)
    act_ref[...] = act_new

    @pl.when(jp == n_exp_tiles - 1)
    def _():
        def step(r, carry):
            rows = pl.ds(pl.multiple_of(r * NORM_ROWS, NORM_ROWS), NORM_ROWS)
            o_ref[rows, :] = _rms(x1_ref[rows, :] + o_ref[rows, :], fg_ref[...])
            return carry

        lax.fori_loop(0, tm // NORM_ROWS, step, 0)


def _peer_ffn(hn_t, u_bf, v_bf, tables, x1, final_g, *, tm, te):
    d, t = hn_t.shape
    n_exp = u_bf.shape[0]
    tm = min(tm, t)
    rows_per_tile = te // N_KEYS
    assert SUBLANES % rows_per_tile == 0 and t % tm == 0 and n_exp % te == 0
    ni, nj = t // tm, n_exp // te
    last = ni * nj - 1

    def cur(s):
        s = jnp.minimum(s, last)
        return s // nj, s % nj

    def prev(s):
        s = jnp.maximum(s - 1, 0)
        return s // nj, s % nj

    tab_spec = pl.BlockSpec((PEER_HEADS, N_KEYS, tm), lambda s: (0, 0, prev(s)[0]))
    row_spec = pl.BlockSpec((PEER_HEADS, SUBLANES, tm),
                            lambda s: (0, (prev(s)[1] * rows_per_tile) // SUBLANES, prev(s)[0]))
    return pl.pallas_call(
        functools.partial(_peer_ffn_kernel, rows_per_tile=rows_per_tile, n_exp_tiles=nj),
        grid=(ni * nj + 1,),
        in_specs=[
            pl.BlockSpec((d, tm), lambda s: (0, cur(s)[0])),
            pl.BlockSpec((te, d), lambda s: (cur(s)[1], 0)),
            pl.BlockSpec((te, d), lambda s: (prev(s)[1], 0)),
            row_spec, row_spec, tab_spec, tab_spec,
            _row_tile_spec((tm, d), lambda s: (prev(s)[0], 0), True),
            pl.BlockSpec((1, d), lambda s: (0, 0)),
        ],
        out_specs=pl.BlockSpec((tm, d), lambda s: (prev(s)[0], 0)),
        out_shape=jax.ShapeDtypeStruct((t, d), F32),
        scratch_shapes=[pltpu.VMEM((te, tm), F32), pltpu.VMEM((te, tm), BF16)],
        compiler_params=_cparams(("arbitrary",)),
        name="peer_ffn",
    )(hn_t, u_bf, v_bf, *tables, x1, final_g)


def _group(x, hist, cache_kv, p, *, n_streams):
    c = p["w_dw"].shape[1]
    a = p["g_attn"].shape[1]
    n_heads = a // HEAD_DIM
    tl = TILES
    z = _norm_matmul(x, p["mix_g"], p["w_in"], p["b_in"], tm=tl["proj_rows"], tn=tl["proj_cols"],
                     single_buffer=True)
    pending = (p["w_out"], p["w_q"]) if p["w_out"].dtype != BF16 else ()
    u, conv_n, done = _conv_module(z, hist, p["w_dw"], p["b_dw"], p["ln_g"], p["ln_b"], p["g_conv"],
                                   n_streams=n_streams, tt=tl["conv_rows"], to_bf16=pending)
    if done:
        p = dict(p, w_out=done[0], w_q=done[1])
    if cache_kv is None:
        blk = 2 * c // HEAD_DIM
        attn = _attn_prompt(z, p["frow"], n_heads=n_heads, q_blk=blk, k_blk=blk + n_heads, v_blk=blk + 2 * n_heads,
                            tq=tl["attn_rows"])
    else:
        blk = 2 * c // a
        attn = _attn_sample(z, cache_kv[0], cache_kv[1], p["frow"], n_streams=n_streams, n_heads=n_heads,
                            q_blk=blk, k_blk=blk + 1, v_blk=blk + 2)
    x1 = _merge(conv_n, attn, p["g_attn"], p["w_out"], x, tm=tl["proj_rows"], tn=tl["proj_cols"])
    q, hn_t = _norm_matmul(x1, p["ffn_g"], p["w_q"], p["zero_bq"], tm=tl["proj_rows"], tn=tl["proj_cols"],
                         emit_norm=True, single_buffer=True)
    pending = (p["peer_u"], p["peer_v"]) if p["peer_u"].dtype != BF16 else ()
    tables, done = _peer_select(q, p["sub_keys"], tq=tl["select_tokens"], to_bf16=pending)
    if done:
        p = dict(p, peer_u=done[0], peer_v=done[1])
    y = _peer_ffn(hn_t, p["peer_u"], p["peer_v"], tables, x1, p["final_g"], tm=tl["ffn_tokens"], te=tl["ffn_experts"])
    return y, z, u, p


def kernel(x_prompt, x_sample, cache_conv, cache_k, cache_v, mix_norm_g, w_in, b_in, w_dw, b_dw, ln_g, ln_b,
           rel_bias, out_norm_conv_g, out_norm_attn_g, w_out, ffn_norm_g, peer_w_q, peer_sub_keys, peer_u, peer_v,
           final_norm_g):
    assert mix_norm_g.shape[0] == 1, "single-layer step"
    bsz, seq, d = x_prompt.shape
    db, ds, _ = x_sample.shape
    c = w_dw.shape[2]
    a = out_norm_attn_g.shape[1]
    n_heads = a // HEAD_DIM
    assert bsz == 1 and seq % ATTN_Q_TILE == 0 and ds >= CONV_WIDTH - 1

    row = lambda v: v.reshape(1, -1)
    p = dict(
        mix_g=row(mix_norm_g[0]), w_in=w_in[0], b_in=row(b_in[0]),
        w_dw=w_dw[0], b_dw=row(b_dw[0]), ln_g=row(ln_g[0]), ln_b=row(ln_b[0]),
        g_conv=row(out_norm_conv_g[0]), g_attn=row(out_norm_attn_g[0]), w_out=w_out[0],
        ffn_g=row(ffn_norm_g[0]), w_q=peer_w_q[0],
        zero_bq=jnp.zeros((1, peer_w_q.shape[2]), F32),
        sub_keys=peer_sub_keys[0].astype(BF16), peer_u=peer_u[0], peer_v=peer_v[0],
        final_g=row(final_norm_g),
        frow=rel_bias[0][:, _bias_row_index()].reshape(n_heads, 1, TOEPLITZ_N),
    )

    hist_p = jnp.zeros((1, CONV_HALO, c), F32)
    y_p, z_p, u_p, p = _group(x_prompt.reshape(seq, d), hist_p, None, p, n_streams=1)

    pad = CONV_HALO - (CONV_WIDTH - 1)
    hist_s = jnp.pad(cache_conv[0], ((0, 0), (pad, 0), (0, 0)))
    r_len = cache_k.shape[2]
    cache_kv = (cache_k[0].reshape(db, r_len * n_heads, HEAD_DIM), cache_v[0].reshape(db, r_len * n_heads, HEAD_DIM))
    y_s, z_s, u_s, _ = _group(x_sample.reshape(db * ds, d), hist_s, cache_kv, p, n_streams=db)

    k0, v0 = 2 * c + a, 2 * c + 2 * a
    rows_p = min(BAND_ROWS, seq)
    keep = CONV_WIDTH - 1
    return (
        y_p.reshape(1, seq, d),
        y_s.reshape(db, ds, d),
        u_p[seq - keep:].reshape(1, 1, keep, c),
        z_p[seq - rows_p:, k0:k0 + a].reshape(1, 1, rows_p, n_heads, HEAD_DIM),
        z_p[seq - rows_p:, v0:v0 + a].reshape(1, 1, rows_p, n_heads, HEAD_DIM),
        u_s.reshape(db, ds, c)[:, ds - keep:].reshape(1, db, keep, c),
        z_s[:, k0:k0 + a].reshape(1, db, ds, n_heads, HEAD_DIM),
        z_s[:, v0:v0 + a].reshape(1, db, ds, n_heads, HEAD_DIM),
    )
```
